```python
import math
import jax, jax.numpy as jnp
from jax import lax
import numpy as np

D_MODEL = 1024
BATCH = 16
SEQ = 2048
DEPTH = 1

HEAD_DIM = 64
SB_HEADS = 8
SB_WIDTH = SB_HEADS * HEAD_DIM
RWKV_HEADS = 8
RWKV_WIDTH = RWKV_HEADS * HEAD_DIM
DECAY_RANK = 64
ICLR_RANK = 64
GATE_RANK = 128
RWKV_COLS = 3 * RWKV_WIDTH + DECAY_RANK + ICLR_RANK + GATE_RANK
IN_COLS = 3 * SB_WIDTH + RWKV_COLS + 2 * D_MODEL
PLE_DIM = 256
N_EXPERTS = 32
TOP_K = 4
EXPERT_HIDDEN = D_MODEL
SWIGLU_LIMIT = 7.0
SWIGLU_ALPHA = 1.702
Q_BLOCK = 128
MOE_BLOCK = 256
RMS_EPS = 1e-5
GN_EPS = 64e-5

kernel_name = "hybrid_stickbreak_rwkv7_moe_block"


def rmsnorm(x, g):
    xf = x.astype(jnp.float32)
    y = xf * lax.rsqrt(jnp.mean(xf * xf, axis=-1, keepdims=True) + RMS_EPS)
    return (y * g.astype(jnp.float32)).astype(x.dtype)


def stick_breaking_attention(q, k, v):
    S = q.shape[2]
    scale = HEAD_DIM ** -0.5
    outs = []
    for blk in range(S // Q_BLOCK):
        t0 = blk * Q_BLOCK
        kl = t0 + Q_BLOCK
        qb = q[:, :, t0:kl]
        kb = k[:, :, :kl]
        vb = v[:, :, :kl]
        z = jnp.einsum('bhqd,bhkd->bhqk', qb, kb).astype(jnp.float32) * scale
        t_idx = t0 + jnp.arange(Q_BLOCK)[:, None]
        s_idx = jnp.arange(kl)[None, :]
        mask = s_idx < t_idx
        log1m = jnp.where(mask, -jax.nn.softplus(z), 0.0)
        after = lax.cumsum(log1m, axis=3, reverse=True) - log1m
        log_a = jax.nn.log_sigmoid(z) + after
        att = jnp.where(mask, jnp.exp(log_a), 0.0)
        outs.append(jnp.einsum('bhqk,bhkd->bhqd', att.astype(vb.dtype), vb))
    return jnp.concatenate(outs, axis=2)


def rwkv7_scan(r, w, k, v, kk, a):
    B, S, H, N = r.shape

    def step(state, inp):
        r_t, w_t, k_t, v_t, kk_t, a_t = inp
        sa = jnp.einsum('bhij,bhj->bhi', state, -kk_t)
        state = (state * w_t[:, :, None, :]
                 + sa[..., None] * (kk_t * a_t)[:, :, None, :]
                 + v_t[..., None] * k_t[:, :, None, :])
        y = jnp.einsum('bhij,bhj->bhi', state, r_t)
        return state, y

    xs = tuple(jnp.moveaxis(t, 1, 0) for t in (r, w, k, v, kk, a))
    state0 = jnp.zeros((B, H, N, N), jnp.float32)
    _, ys = lax.scan(step, state0, xs)
    return jnp.moveaxis(ys, 0, 1)


def rwkv7_branch(zr, mu, w0, w_up, a0, a_up, g_up, k_k, k_a, r_k, lnx_w, lnx_b):
    B, S, _ = zr.shape
    prev = jnp.pad(zr, ((0, 0), (1, 0), (0, 0)))[:, :-1]
    zr = zr + mu * (prev - zr)
    c1, c2, c3 = RWKV_WIDTH, 2 * RWKV_WIDTH, 3 * RWKV_WIDTH
    c4, c5 = c3 + DECAY_RANK, c3 + DECAY_RANK + ICLR_RANK
    r, k, v = zr[..., :c1], zr[..., c1:c2], zr[..., c2:c3]
    xw, xa, xg = zr[..., c3:c4], zr[..., c4:c5], zr[..., c5:]
    w_log = -jax.nn.softplus(-(w0 + jnp.tanh(xw) @ w_up)) - 0.5
    decay = jnp.exp(-jnp.exp(w_log.astype(jnp.float32)))
    a = jax.nn.sigmoid(a0 + xa @ a_up)
    g = jax.nn.sigmoid(xg) @ g_up
    kk = k * k_k
    hs = lambda t: t.astype(jnp.float32).reshape(B, S, RWKV_HEADS, HEAD_DIM)
    r_h, k_h, v_h, kk_h, a_h, w_h = hs(r), hs(k), hs(v), hs(kk), hs(a), hs(decay)
    kk_h = kk_h / jnp.maximum(jnp.linalg.norm(kk_h, axis=-1, keepdims=True), 1e-12)
    k_h = k_h * (1.0 + (a_h - 1.0) * hs(k_a * jnp.ones_like(k)))
    y = rwkv7_scan(r_h, w_h, k_h, v_h, kk_h, a_h)
    mean = jnp.mean(y, axis=-1, keepdims=True)
    var = jnp.mean(jnp.square(y - mean), axis=-1, keepdims=True)
    y = (y - mean) * lax.rsqrt(var + GN_EPS)
    y = y.reshape(B, S, RWKV_WIDTH) * lnx_w + lnx_b
    bonus = jnp.sum(r_h * k_h * r_k.astype(jnp.float32), axis=-1, keepdims=True) * v_h
    y = y + bonus.reshape(B, S, RWKV_WIDTH)
    return (y * g).astype(zr.dtype)


def moe_ffn(h, router_w, router_b, w1, b1, w2, b2):
    Bsz, S, D = h.shape
    N = Bsz * S
    NK = N * TOP_K
    hf = h.reshape(N, D)
    logits = (hf @ router_w + router_b).astype(jnp.float32)
    top_val, top_idx = lax.top_k(logits, TOP_K)
    gate = jax.nn.softmax(top_val, axis=-1)
    flat_e = top_idx.reshape(-1).astype(jnp.int32)
    flat_tok = jnp.arange(NK, dtype=jnp.int32) // TOP_K
    flat_w = gate.reshape(-1)
    order = jnp.argsort(flat_e)
    sorted_e = flat_e[order]
    counts = jnp.zeros((N_EXPERTS,), jnp.int32).at[flat_e].add(1)
    offsets = jnp.cumsum(counts) - counts
    padded = (counts + MOE_BLOCK - 1) // MOE_BLOCK * MOE_BLOCK
    pad_end = jnp.cumsum(padded)
    pad_off = pad_end - padded
    rank = jnp.arange(NK, dtype=jnp.int32) - offsets[sorted_e]
    dest = pad_off[sorted_e] + rank
    n_blocks = -(-NK // MOE_BLOCK) + N_EXPERTS
    P = n_blocks * MOE_BLOCK
    buf_tok = jnp.zeros((P,), jnp.int32).at[dest].set(flat_tok[order])
    buf_w = jnp.zeros((P,), jnp.float32).at[dest].set(flat_w[order])
    block_start = jnp.arange(n_blocks, dtype=jnp.int32) * MOE_BLOCK
    block_e = jnp.minimum(jnp.searchsorted(pad_end, block_start, side='right'), N_EXPERTS - 1)

    def expert_block(args):
        e, tok = args
        xb = hf[tok]
        gu = xb @ w1[e] + b1[e]
        g_lin, u_lin = gu[:, ::2], gu[:, 1::2]
        g_lin = jnp.minimum(g_lin, SWIGLU_LIMIT)
        u_lin = jnp.clip(u_lin, -SWIGLU_LIMIT, SWIGLU_LIMIT)
        glu = g_lin * jax.nn.sigmoid(SWIGLU_ALPHA * g_lin)
        return ((u_lin + 1.0) * glu) @ w2[e] + b2[e]

    y = lax.map(expert_block, (block_e, buf_tok.reshape(n_blocks, MOE_BLOCK)))
    y = y.reshape(P, D) * buf_w[:, None].astype(y.dtype)
    out = jnp.zeros((N, D), h.dtype).at[buf_tok].add(y.astype(h.dtype))
    return out.reshape(Bsz, S, D)


def setup_inputs(seed: int = 0) -> dict:
    key = jax.random.key(seed)
    ks = jax.random.split(key, 32)
    L, D, E, F = DEPTH, D_MODEL, N_EXPERTS, EXPERT_HIDDEN
    nrm = lambda k, shape, s: jax.random.normal(k, shape, jnp.float32) * s
    return {
        "x": nrm(ks[0], (BATCH, SEQ, D), 1.0),
        "p": nrm(ks[1], (L, BATCH, SEQ, PLE_DIM), 1.0),
        "mix_norm_g": 1.0 + nrm(ks[2], (L, D), 0.02),
        "w_in": nrm(ks[3], (L, D, IN_COLS), D ** -0.5),
        "rwkv_mu": jax.random.uniform(ks[4], (L, RWKV_COLS), jnp.float32),
        "rwkv_w0": jax.random.uniform(ks[5], (L, RWKV_WIDTH), jnp.float32, -4.0, 1.0),
        "rwkv_w_up": nrm(ks[6], (L, DECAY_RANK, RWKV_WIDTH), 0.5 * DECAY_RANK ** -0.5),
        "rwkv_a0": nrm(ks[7], (L, RWKV_WIDTH), 0.1),
        "rwkv_a_up": nrm(ks[8], (L, ICLR_RANK, RWKV_WIDTH), ICLR_RANK ** -0.5),
        "rwkv_g_up": nrm(ks[9], (L, GATE_RANK, RWKV_WIDTH), GATE_RANK ** -0.5),
        "rwkv_k_k": 0.85 + nrm(ks[10], (L, RWKV_WIDTH), 0.05),
        "rwkv_k_a": 1.0 + nrm(ks[11], (L, RWKV_WIDTH), 0.05),
        "rwkv_r_k": nrm(ks[12], (L, RWKV_HEADS, HEAD_DIM), 0.1),
        "rwkv_lnx_w": 1.0 + nrm(ks[13], (L, RWKV_WIDTH), 0.02),
        "rwkv_lnx_b": nrm(ks[14], (L, RWKV_WIDTH), 0.01),
        "w_out_a": nrm(ks[15], (L, SB_WIDTH, D), SB_WIDTH ** -0.5),
        "w_out_b": nrm(ks[16], (L, RWKV_WIDTH, D), RWKV_WIDTH ** -0.5),
        "w_out": nrm(ks[17], (L, D, D), D ** -0.5),
        "ffn_norm_g": 1.0 + nrm(ks[18], (L, D), 0.02),
        "router_w": nrm(ks[19], (L, D, E), D ** -0.5),
        "router_b": nrm(ks[20], (L, E), 0.01),
        "exp_w1": nrm(ks[21], (L, E, D, 2 * F), D ** -0.5),
        "exp_b1": nrm(ks[22], (L, E, 2 * F), 0.01),
        "exp_w2": nrm(ks[23], (L, E, F, D), F ** -0.5),
        "exp_b2": nrm(ks[24], (L, E, D), 0.01),
        "ple_norm_g": 1.0 + nrm(ks[25], (L, D), 0.02),
        "ple_gate_w": nrm(ks[26], (L, D, D), D ** -0.5),
        "ple_proj_w": nrm(ks[27], (L, PLE_DIM, D), PLE_DIM ** -0.5),
        "final_norm_g": 1.0 + nrm(ks[28], (D,), 0.02),
    }


def reference(x, p, mix_norm_g, w_in, rwkv_mu, rwkv_w0, rwkv_w_up, rwkv_a0, rwkv_a_up,
              rwkv_g_up, rwkv_k_k, rwkv_k_a, rwkv_r_k, rwkv_lnx_w, rwkv_lnx_b,
              w_out_a, w_out_b, w_out, ffn_norm_g, router_w, router_b,
              exp_w1, exp_b1, exp_w2, exp_b2, ple_norm_g, ple_gate_w, ple_proj_w,
              final_norm_g):
    B, S, D = x.shape
    for i in range(DEPTH):
        h = rmsnorm(x, mix_norm_g[i])
        z = h @ w_in[i]
        c_sb = 3 * SB_WIDTH
        c_rw = c_sb + RWKV_COLS
        zq, zk, zv = (z[..., j * SB_WIDTH:(j + 1) * SB_WIDTH] for j in range(3))
        to_heads = lambda t: t.reshape(B, S, SB_HEADS, HEAD_DIM).transpose(0, 2, 1, 3)
        ya = stick_breaking_attention(to_heads(zq), to_heads(zk), to_heads(zv))
        ya = ya.transpose(0, 2, 1, 3).reshape(B, S, SB_WIDTH)
        yb = rwkv7_branch(z[..., c_sb:c_rw], rwkv_mu[i], rwkv_w0[i], rwkv_w_up[i], rwkv_a0[i],
                          rwkv_a_up[i], rwkv_g_up[i], rwkv_k_k[i], rwkv_k_a[i], rwkv_r_k[i],
                          rwkv_lnx_w[i], rwkv_lnx_b[i])
        gate_a = jax.nn.sigmoid(z[..., c_rw:c_rw + D])
        gate_b = jax.nn.sigmoid(z[..., c_rw + D:])
        merged = gate_a * (ya @ w_out_a[i]) + gate_b * (yb @ w_out_b[i])
        x = x + (merged @ w_out[i]).astype(x.dtype)
        x = x + moe_ffn(rmsnorm(x, ffn_norm_g[i]), router_w[i], router_b[i],
                        exp_w1[i], exp_b1[i], exp_w2[i], exp_b2[i])
        hp = rmsnorm(x, ple_norm_g[i])
        x = x + (jax.nn.sigmoid(hp @ ple_gate_w[i]) * (p[i] @ ple_proj_w[i])).astype(x.dtype)
    return rmsnorm(x, final_norm_g)
```

```python
import functools

import jax
import jax.numpy as jnp
from jax import lax
from jax.experimental import pallas as pl
from jax.experimental.pallas import tpu as pltpu

F32 = jnp.float32
BF16 = jnp.bfloat16

D_MODEL = 1024
HEAD_DIM = 64
SB_WIDTH = 512
RWKV_WIDTH = 512
DECAY_RANK = 64
ICLR_RANK = 64
GATE_RANK = 128
RWKV_COLS = 3 * RWKV_WIDTH + DECAY_RANK + ICLR_RANK + GATE_RANK
PLE_DIM = 256
N_EXPERTS = 32
TOP_K = 4
EXPERT_HIDDEN = D_MODEL
SWIGLU_LIMIT = 7.0
SWIGLU_ALPHA = 1.702
MOE_BLOCK = 256
RMS_EPS = 1e-5
GN_EPS = 64e-5

LANES = 128
ROW_TILE = 256
ATTN_BLOCK = 256
CHUNK = 64
VMEM_LIMIT = 56 * 1024 * 1024


def _params(sem, vmem=VMEM_LIMIT):
    return pltpu.CompilerParams(dimension_semantics=sem, vmem_limit_bytes=vmem)


def _dot(a, b):
    return jnp.dot(a.astype(BF16), b.astype(BF16), preferred_element_type=F32)


def _dot_nt(a, b):
    return lax.dot_general(a.astype(BF16), b.astype(BF16), (((1,), (1,)), ((), ())),
                           preferred_element_type=F32)


def _split(a):
    hi = a.astype(BF16)
    lo = (a - hi.astype(F32)).astype(BF16)
    return hi, lo


def _dot_exact_lhs(mask_bf16, a):
    hi, lo = _split(a)
    return (jnp.dot(mask_bf16, hi, preferred_element_type=F32)
            + jnp.dot(mask_bf16, lo, preferred_element_type=F32))


def _dot_exact_rhs(a, mask_bf16):
    hi, lo = _split(a)
    return (jnp.dot(hi, mask_bf16, preferred_element_type=F32)
            + jnp.dot(lo, mask_bf16, preferred_element_type=F32))


def _softplus(z):
    return jnp.maximum(z, 0.0) + jnp.log(1.0 + jnp.exp(-jnp.abs(z)))


def _sigmoid(z):
    return 1.0 / (1.0 + jnp.exp(-z))


def _rms(x, g):
    ms = jnp.mean(x * x, axis=-1, keepdims=True)
    return x * lax.rsqrt(ms + RMS_EPS) * g


def _inproj_kernel(x_ref, g_ref, wqkv_ref, wr_ref, wg_ref, qkv_ref, zr_ref, zg_ref):
    h = _rms(x_ref[...], g_ref[...]).astype(BF16)
    qkv_ref[...] = jnp.dot(h, wqkv_ref[...], preferred_element_type=F32).astype(BF16)
    zr_ref[...] = jnp.dot(h, wr_ref[...], preferred_element_type=F32)
    zg_ref[...] = jnp.dot(h, wg_ref[...], preferred_element_type=F32)


def _inproj(x2, g, wqkv, wr, wg):
    n = x2.shape[0]
    tm = ROW_TILE
    full = lambda a: pl.BlockSpec(a.shape, lambda i: (0,) * a.ndim)
    rows = lambda c: pl.BlockSpec((tm, c), lambda i: (i, 0))
    return pl.pallas_call(
        _inproj_kernel,
        grid=(n // tm,),
        in_specs=[rows(D_MODEL), full(g), full(wqkv), full(wr), full(wg)],
        out_specs=[rows(3 * SB_WIDTH), rows(RWKV_COLS), rows(2 * D_MODEL)],
        out_shape=[jax.ShapeDtypeStruct((n, 3 * SB_WIDTH), BF16),
                   jax.ShapeDtypeStruct((n, RWKV_COLS), F32),
                   jax.ShapeDtypeStruct((n, 2 * D_MODEL), F32)],
        compiler_params=_params(("parallel",)),
        name="inproj",
    )(x2, g, wqkv, wr, wg)


def _sbattn_kernel(q_ref, k_ref, v_ref, o_ref):
    tb = ATTN_BLOCK
    i = pl.program_id(2)
    q = q_ref[0]
    lane = lax.broadcasted_iota(jnp.int32, (tb, LANES), 1)
    row = lax.broadcasted_iota(jnp.int32, (tb, tb), 0)
    col = lax.broadcasted_iota(jnp.int32, (tb, tb), 1)
    causal = col < row
    later = jnp.where(row > col, 1.0, 0.0).astype(BF16)

    def block(qh, j, carry, acc, diag):
        kb = k_ref[0, pl.ds(pl.multiple_of(j * tb, tb), tb), :]
        vb = v_ref[0, pl.ds(pl.multiple_of(j * tb, tb), tb), :]
        z = lax.dot_general(qh, kb, (((1,), (1,)), ((), ())), preferred_element_type=F32)
        sp = _softplus(z)
        l1 = -sp
        if diag:
            l1 = jnp.where(causal, l1, 0.0)
        after = _dot_exact_rhs(l1, later)
        att = jnp.exp((z - sp) + after + carry)
        if diag:
            att = jnp.where(causal, att, 0.0)
        acc = acc + jnp.dot(att.astype(BF16), vb, preferred_element_type=F32)
        carry = carry + jnp.sum(l1, axis=1, keepdims=True)
        return carry, acc

    outs = []
    for head in range(2):
        in_head = (lane < HEAD_DIM) if head == 0 else (lane >= HEAD_DIM)
        qh = jnp.where(in_head, q, jnp.zeros_like(q))
        carry, acc = block(qh, i, jnp.zeros((tb, 1), F32), jnp.zeros((tb, LANES), F32), True)

        def body(t, c, qh=qh):
            return block(qh, i - 1 - t, c[0], c[1], False)

        carry, acc = lax.fori_loop(0, i, body, (carry, acc))
        outs.append(acc)
    o_ref[0] = jnp.where(lane < HEAD_DIM, outs[0], outs[1]).astype(BF16)


def _sbattn(qkv3):
    b, s, _ = qkv3.shape
    tb = ATTN_BLOCK
    pairs = SB_WIDTH // LANES
    return pl.pallas_call(
        _sbattn_kernel,
        grid=(b, pairs, s // tb),
        in_specs=[pl.BlockSpec((1, tb, LANES), lambda bb, p, i: (bb, i, p)),
                  pl.BlockSpec((1, s, LANES), lambda bb, p, i: (bb, 0, pairs + p)),
                  pl.BlockSpec((1, s, LANES), lambda bb, p, i: (bb, 0, 2 * pairs + p))],
        out_specs=pl.BlockSpec((1, tb, LANES), lambda bb, p, i: (bb, i, p)),
        out_shape=jax.ShapeDtypeStruct((b, s, SB_WIDTH), BF16),
        compiler_params=_params(("parallel", "parallel", "arbitrary")),
        name="sbattn",
    )(qkv3, qkv3, qkv3)


def _head_sums(t):
    lane = lax.broadcasted_iota(jnp.int32, t.shape, 1)
    low = lane < HEAD_DIM
    s_low = jnp.sum(jnp.where(low, t, 0.0), axis=1, keepdims=True)
    s_high = jnp.sum(jnp.where(low, 0.0, t), axis=1, keepdims=True)
    return jnp.where(low, s_low, s_high)


def _rwprep_kernel(z_ref, zp_ref, mu_ref, w0_ref, wup_ref, a0_ref, aup_ref, gup_ref, kk_ref, ka_ref,
                   r_ref, k_ref, v_ref, kkn_ref, a_ref, lw_ref, g_ref):
    ts = z_ref.shape[1]
    i = pl.program_id(1)
    z = z_ref[0]
    rowid = lax.broadcasted_iota(jnp.int32, z.shape, 0)
    last = jnp.where(i == 0, 0.0, zp_ref[0, 7:8, :])
    prev = jnp.where(rowid == 0, last, pltpu.roll(z, 1, 0))
    zs = z + mu_ref[...] * (prev - z)
    c1, c2, c3 = RWKV_WIDTH, 2 * RWKV_WIDTH, 3 * RWKV_WIDTH
    r, k, v = zs[:, :c1], zs[:, c1:c2], zs[:, c2:c3]
    xwa = zs[:, c3:c3 + LANES]
    xg = zs[:, c3 + LANES:]
    hp = lax.Precision.HIGHEST
    w_pre = w0_ref[...] + jnp.dot(jnp.tanh(xwa), wup_ref[...], precision=hp, preferred_element_type=F32)
    w_log = -_softplus(-w_pre) - 0.5
    lw = -jnp.exp(w_log)
    a = _sigmoid(a0_ref[...] + jnp.dot(xwa, aup_ref[...], precision=hp, preferred_element_type=F32))
    g = jnp.dot(_sigmoid(xg), gup_ref[...], precision=hp, preferred_element_type=F32)
    kk = k * kk_ref[...]
    for t in range(RWKV_WIDTH // LANES):
        sl = slice(t * LANES, (t + 1) * LANES)
        kt = kk[:, sl]
        nrm = jnp.sqrt(_head_sums(kt * kt))
        kkn_ref[0, :, sl] = kt / jnp.maximum(nrm, 1e-12)
    r_ref[0] = r
    k_ref[0] = k * (1.0 + (a - 1.0) * ka_ref[...])
    v_ref[0] = v
    a_ref[0] = a
    lw_ref[0] = lw
    g_ref[0] = g


def _rwprep(zr3, mu, w0, wup, a0, aup, gup, k_k, k_a):
    b, s, _ = zr3.shape
    ts = ROW_TILE
    full = lambda a: pl.BlockSpec(a.shape, lambda bb, i: (0,) * a.ndim)
    out = pl.BlockSpec((1, ts, RWKV_WIDTH), lambda bb, i: (bb, i, 0))
    return pl.pallas_call(
        _rwprep_kernel,
        grid=(b, s // ts),
        in_specs=[pl.BlockSpec((1, ts, RWKV_COLS), lambda bb, i: (bb, i, 0)),
                  pl.BlockSpec((1, 8, RWKV_COLS), lambda bb, i: (bb, jnp.maximum(i * (ts // 8) - 1, 0), 0)),
                  full(mu), full(w0), full(wup), full(a0), full(aup), full(gup), full(k_k), full(k_a)],
        out_specs=[out] * 7,
        out_shape=[jax.ShapeDtypeStruct((b, s, RWKV_WIDTH), F32)] * 7,
        compiler_params=_params(("parallel", "parallel")),
        name="rwprep",
    )(zr3, zr3, mu, w0, wup, a0, aup, gup, k_k, k_a)


def _unit_lower_inverse(lb):
    n = lb.shape[0]
    eye = jnp.where(lax.broadcasted_iota(jnp.int32, (n, n), 0) == lax.broadcasted_iota(jnp.int32, (n, n), 1),
                    1.0, 0.0)
    p = -lb
    t = eye + p
    steps = n.bit_length() - 1
    for _ in range(steps - 1):
        p = _dot(p, p)
        t = t + _dot(t, p)
    return t


def _rwscan_kernel(r_ref, k_ref, v_ref, kk_ref, a_ref, lw_ref, g_ref, lnw_ref, lnb_ref, rk_ref, o_ref):
    c = CHUNK
    s_len = r_ref.shape[1]
    lane = lax.broadcasted_iota(jnp.int32, (c, LANES), 1)
    heads = (lane < HEAD_DIM, lane >= HEAD_DIM)
    trow = lax.broadcasted_iota(jnp.int32, (c, c), 0)
    tcol = lax.broadcasted_iota(jnp.int32, (c, c), 1)
    strict = tcol < trow
    incl = tcol <= trow
    cum = jnp.where(incl, 1.0, 0.0).astype(BF16)
    srow = lax.broadcasted_iota(jnp.int32, (LANES, LANES), 0) < HEAD_DIM
    scol = lax.broadcasted_iota(jnp.int32, (LANES, LANES), 1) < HEAD_DIM
    same_head = srow == scol
    lnw, lnb, rk = lnw_ref[...], lnb_ref[...], rk_ref[...]

    def chunk(ci, state):
        sl = pl.ds(pl.multiple_of(ci * c, c), c)
        r, k, v = r_ref[0, sl, :], k_ref[0, sl, :], v_ref[0, sl, :]
        kk, a, lw = kk_ref[0, sl, :], a_ref[0, sl, :], lw_ref[0, sl, :]
        lg = _dot_exact_lhs(cum, lw)
        lg_end = lg[c - 1:c, :]
        e_in = jnp.exp(lg)
        e_ex = jnp.exp(lg - lw)
        e_inv = jnp.exp(-lg)
        e_end = jnp.exp(lg_end - lg)
        ab = kk * a
        kk_t = kk * e_ex
        r_t = r * e_in
        b_t = ab * e_inv
        k_t = k * e_inv
        b_end = ab * e_end
        k_end = k * e_end
        w_all = jnp.zeros((c, LANES), F32)
        uv_all = jnp.zeros((c, LANES), F32)
        yv_all = jnp.zeros((c, LANES), F32)
        arb = []
        for hm in heads:
            kk_h = jnp.where(hm, kk_t, 0.0)
            r_h = jnp.where(hm, r_t, 0.0)
            v_h = jnp.where(hm, v, 0.0)
            l_b = jnp.where(strict, _dot_nt(kk_h, b_t), 0.0)
            l_k = jnp.where(strict, _dot_nt(kk_h, k_t), 0.0)
            a_rb = jnp.where(incl, _dot_nt(r_h, b_t), 0.0)
            a_rk = jnp.where(incl, _dot_nt(r_h, k_t), 0.0)
            t_inv = _unit_lower_inverse(l_b)
            w_all = w_all + _dot(t_inv, kk_h)
            uv_all = uv_all + _dot(t_inv, _dot(l_k, v_h))
            yv_all = yv_all + _dot(a_rk, v_h)
            arb.append(a_rb)
        zs = _dot_nt(jnp.concatenate([w_all, r_t], axis=0), state)
        u = -(zs[:c] + uv_all)
        y = zs[c:] + yv_all
        for hm, a_rb in zip(heads, arb):
            y = y + _dot(a_rb, jnp.where(hm, u, 0.0))
        upd = _dot(jnp.concatenate([u, v], axis=0).T, jnp.concatenate([b_end, k_end], axis=0))
        state = state * jnp.exp(lg_end) + jnp.where(same_head, upd, 0.0)
        mean = _head_sums(y) * (1.0 / HEAD_DIM)
        d = y - mean
        var = _head_sums(d * d) * (1.0 / HEAD_DIM)
        yn = d * lax.rsqrt(var + GN_EPS) * lnw + lnb
        bonus = _head_sums(r * k * rk) * v
        o_ref[0, sl, :] = ((yn + bonus) * g_ref[0, sl, :]).astype(o_ref.dtype)
        return state

    lax.fori_loop(0, s_len // c, chunk, jnp.zeros((LANES, LANES), F32))


def _rwscan(r, k, v, kk, a, lw, g, lnw, lnb, rk):
    b, s, _ = r.shape
    pairs = RWKV_WIDTH // LANES
    seq = pl.BlockSpec((1, s, LANES), lambda bb, p: (bb, 0, p))
    vec = pl.BlockSpec((1, LANES), lambda bb, p: (0, p))
    return pl.pallas_call(
        _rwscan_kernel,
        grid=(b, pairs),
        in_specs=[seq] * 7 + [vec] * 3,
        out_specs=seq,
        out_shape=jax.ShapeDtypeStruct((b, s, RWKV_WIDTH), BF16),
        compiler_params=_params(("parallel", "parallel")),
        name="rwscan",
    )(r, k, v, kk, a, lw, g, lnw, lnb, rk)


def _mixout_kernel(x_ref, ya_ref, yb_ref, zg_ref, woa_ref, wob_ref, wo_ref, fg_ref, rwt_ref, rb_ref,
                   x1_ref, h2_ref, idx_ref, gate_ref):
    ga = _sigmoid(zg_ref[:, :D_MODEL])
    gb = _sigmoid(zg_ref[:, D_MODEL:])
    merged = (ga * jnp.dot(ya_ref[...], woa_ref[...], preferred_element_type=F32)
              + gb * jnp.dot(yb_ref[...], wob_ref[...], preferred_element_type=F32))
    x1 = x_ref[...] + jnp.dot(merged.astype(BF16), wo_ref[...], preferred_element_type=F32)
    x1_ref[...] = x1
    h2 = _rms(x1, fg_ref[...])
    h2_ref[...] = h2
    logits = lax.dot_general(rwt_ref[...], h2, (((1,), (1,)), ((), ())),
                             precision=lax.Precision.HIGHEST, preferred_element_type=F32) + rb_ref[...]
    eid = lax.broadcasted_iota(jnp.int32, logits.shape, 0).astype(F32)
    vals, ids = [], []
    for _ in range(TOP_K):
        m = jnp.max(logits, axis=0, keepdims=True)
        sel = jnp.min(jnp.where(logits == m, eid, float(N_EXPERTS)), axis=0, keepdims=True)
        vals.append(m)
        ids.append(sel.astype(jnp.int32))
        logits = jnp.where(eid == sel, -jnp.inf, logits)
    ex = [jnp.exp(vv - vals[0]) for vv in vals]
    den = ex[0] + ex[1] + ex[2] + ex[3]
    idx_ref[...] = jnp.concatenate(ids, axis=0)
    gate_ref[...] = jnp.concatenate([e / den for e in ex], axis=0)


def _mixout(x2, ya, yb, zg, woa, wob, wo, fg, rwt, rb):
    n = x2.shape[0]
    tm = ROW_TILE
    full = lambda a: pl.BlockSpec(a.shape, lambda i: (0,) * a.ndim)
    rows = lambda c: pl.BlockSpec((tm, c), lambda i: (i, 0))
    cols = pl.BlockSpec((TOP_K, tm), lambda i: (0, i))
    return pl.pallas_call(
        _mixout_kernel,
        grid=(n // tm,),
        in_specs=[rows(D_MODEL), rows(SB_WIDTH), rows(RWKV_WIDTH), rows(2 * D_MODEL),
                  full(woa), full(wob), full(wo), full(fg), full(rwt), full(rb)],
        out_specs=[rows(D_MODEL), rows(D_MODEL), cols, cols],
        out_shape=[jax.ShapeDtypeStruct((n, D_MODEL), F32),
                   jax.ShapeDtypeStruct((n, D_MODEL), F32),
                   jax.ShapeDtypeStruct((TOP_K, n), jnp.int32),
                   jax.ShapeDtypeStruct((TOP_K, n), F32)],
        compiler_params=_params(("parallel",)),
        name="mixout",
    )(x2, ya, yb, zg, woa, wob, wo, fg, rwt, rb)


def _moe_kernel(be_ref, nvalid_ref, nused_ref, tok0_ref, tokn_ref, dst_ref, wrow_ref, h_hbm,
                w1g_ref, w1u_ref, b1g_ref, b1u_ref, w2_ref, b2_ref, y_hbm,
                xbuf, ybuf, gsem, ssem):
    g = MOE_BLOCK
    i = pl.program_id(0)
    n_used = nused_ref[0]
    slot = lax.rem(i, 2)

    def gather(tok_ref, s):
        def body(r, _):
            pltpu.make_async_copy(h_hbm.at[pl.ds(tok_ref[0, 0, r], 1)], xbuf.at[s, pl.ds(r, 1)],
                                  gsem.at[s]).start()
            return 0
        lax.fori_loop(0, g, body, 0, unroll=8)

    @pl.when(jnp.logical_and(i == 0, n_used > 0))
    def _():
        gather(tok0_ref, 0)

    @pl.when(i + 1 < n_used)
    def _():
        gather(tokn_ref, 1 - slot)

    @pl.when(i < n_used)
    def _():
        pltpu.make_async_copy(xbuf.at[slot], xbuf.at[slot], gsem.at[slot]).wait()
        xb = xbuf[slot].astype(BF16)
        g_lin = jnp.dot(xb, w1g_ref[0], preferred_element_type=F32) + b1g_ref[0]
        u_lin = jnp.dot(xb, w1u_ref[0], preferred_element_type=F32) + b1u_ref[0]
        g_lin = jnp.minimum(g_lin, SWIGLU_LIMIT)
        u_lin = jnp.clip(u_lin, -SWIGLU_LIMIT, SWIGLU_LIMIT)
        glu = g_lin * _sigmoid(SWIGLU_ALPHA * g_lin)
        hmid = ((u_lin + 1.0) * glu).astype(BF16)
        y = jnp.dot(hmid, w2_ref[0], preferred_element_type=F32) + b2_ref[0]
        rr = lax.broadcasted_iota(jnp.int32, (g, g), 0)
        cc = lax.broadcasted_iota(jnp.int32, (g, g), 1)
        wdiag = jnp.where(rr == cc, jnp.broadcast_to(wrow_ref[0], (g, g)), 0.0)
        wcol = jnp.dot(wdiag, jnp.ones((g, LANES), F32), precision=lax.Precision.HIGHEST,
                       preferred_element_type=F32)
        ybuf[...] = y * jnp.tile(wcol, (1, D_MODEL // LANES))

        def row_copy(r):
            return pltpu.make_async_copy(ybuf.at[pl.ds(r, 1)], y_hbm.at[pl.ds(dst_ref[0, 0, r], 1)], ssem)

        def start(r, _):
            row_copy(r).start()
            return 0

        def wait(r, _):
            row_copy(r).wait()
            return 0
        lax.fori_loop(0, nvalid_ref[i], start, 0)
        lax.fori_loop(0, nvalid_ref[i], wait, 0)


def _moe(block_e, nvalid, n_used, tok, dst, wrow, h2, w1g, w1u, b1g, b1u, w2, b2, n_assign):
    nb = tok.shape[0]
    g = MOE_BLOCK
    d = D_MODEL
    f = EXPERT_HIDDEN
    smem = lambda im: pl.BlockSpec((1, 1, g), im, memory_space=pltpu.SMEM)
    ew = lambda shape: pl.BlockSpec((1,) + shape, lambda i, be, nv, nu: (be[i], 0, 0))
    grid_spec = pltpu.PrefetchScalarGridSpec(
        num_scalar_prefetch=3,
        grid=(nb,),
        in_specs=[smem(lambda i, be, nv, nu: (0, 0, 0)),
                  smem(lambda i, be, nv, nu: (jnp.minimum(i + 1, nb - 1), 0, 0)),
                  smem(lambda i, be, nv, nu: (i, 0, 0)),
                  pl.BlockSpec((1, 1, g), lambda i, be, nv, nu: (i, 0, 0)),
                  pl.BlockSpec(memory_space=pl.ANY),
                  ew((d, f)), ew((d, f)), ew((1, f)), ew((1, f)), ew((f, d)), ew((1, d))],
        out_specs=pl.BlockSpec(memory_space=pl.ANY),
        scratch_shapes=[pltpu.VMEM((2, g, d), F32), pltpu.VMEM((g, d), F32),
                        pltpu.SemaphoreType.DMA((2,)), pltpu.SemaphoreType.DMA(())],
    )
    return pl.pallas_call(
        _moe_kernel,
        grid_spec=grid_spec,
        out_shape=jax.ShapeDtypeStruct((n_assign, d), F32),
        compiler_params=_params(("arbitrary",)),
        name="moe",
    )(block_e, nvalid, n_used, tok, tok, dst, wrow, h2, w1g, w1u, b1g, b1u, w2, b2)


def _final_kernel(x1_ref, y0_ref, y1_ref, y2_ref, y3_ref, p_ref, pg_ref, wgate_ref, wproj_ref, fg_ref, o_ref,
                  *, last):
    x2 = x1_ref[...] + y0_ref[...] + y1_ref[...] + y2_ref[...] + y3_ref[...]
    hp = _rms(x2, pg_ref[...]).astype(BF16)
    gate = _sigmoid(jnp.dot(hp, wgate_ref[...], preferred_element_type=F32))
    proj = jnp.dot(p_ref[...].astype(BF16), wproj_ref[...], preferred_element_type=F32)
    x3 = x2 + gate * proj
    o_ref[...] = _rms(x3, fg_ref[...]) if last else x3


def _final(x1, ycomb, p2, pg, wgate, wproj, fg, last):
    n = x1.shape[0]
    tm = ROW_TILE
    full = lambda a: pl.BlockSpec(a.shape, lambda i: (0,) * a.ndim)
    rows = lambda c: pl.BlockSpec((tm, c), lambda i: (i, 0))
    ysp = lambda k: pl.BlockSpec((tm, D_MODEL), lambda i: (k * (n // tm) + i, 0))
    return pl.pallas_call(
        functools.partial(_final_kernel, last=last),
        grid=(n // tm,),
        in_specs=[rows(D_MODEL), ysp(0), ysp(1), ysp(2), ysp(3), rows(PLE_DIM),
                  full(pg), full(wgate), full(wproj), full(fg)],
        out_specs=rows(D_MODEL),
        out_shape=jax.ShapeDtypeStruct((n, D_MODEL), F32),
        compiler_params=_params(("parallel",)),
        name="final",
    )(x1, ycomb, ycomb, ycomb, ycomb, p2, pg, wgate, wproj, fg)


def _dispatch(top_idx, gate):
    n = top_idx.shape[0]
    nk = n * TOP_K
    flat_e = top_idx.reshape(-1)
    order = jnp.argsort(flat_e).astype(jnp.int32)
    sorted_e = flat_e[order]
    counts = jnp.zeros((N_EXPERTS,), jnp.int32).at[flat_e].add(1)
    offsets = jnp.cumsum(counts) - counts
    padded = (counts + MOE_BLOCK - 1) // MOE_BLOCK * MOE_BLOCK
    pad_end = jnp.cumsum(padded)
    pad_off = pad_end - padded
    dest = pad_off[sorted_e] + jnp.arange(nk, dtype=jnp.int32) - offsets[sorted_e]
    n_blocks = -(-nk // MOE_BLOCK) + N_EXPERTS
    rows = n_blocks * MOE_BLOCK
    buf_dst = jnp.zeros((rows,), jnp.int32).at[dest].set((order % TOP_K) * n + order // TOP_K)
    buf_tok = jnp.zeros((rows,), jnp.int32).at[dest].set(order // TOP_K)
    buf_w = jnp.zeros((rows,), F32).at[dest].set(gate.reshape(-1)[order])
    block_start = jnp.arange(n_blocks, dtype=jnp.int32) * MOE_BLOCK
    block_e = jnp.minimum(jnp.searchsorted(pad_end, block_start, side='right'), N_EXPERTS - 1)
    n_used = (pad_end[-1] // MOE_BLOCK).astype(jnp.int32).reshape(1)
    nvalid = jnp.clip(counts[block_e] - (block_start - pad_off[block_e]), 0, MOE_BLOCK).astype(jnp.int32)
    return (block_e.astype(jnp.int32), nvalid, n_used, buf_tok.reshape(n_blocks, 1, MOE_BLOCK),
            buf_dst.reshape(n_blocks, 1, MOE_BLOCK), buf_w.reshape(n_blocks, 1, MOE_BLOCK))


def _pad_rows(w, before, total):
    return jnp.zeros((total, w.shape[1]), w.dtype).at[before:before + w.shape[0]].set(w)


def kernel(x, p, mix_norm_g, w_in, rwkv_mu, rwkv_w0, rwkv_w_up, rwkv_a0, rwkv_a_up, rwkv_g_up, rwkv_k_k, rwkv_k_a, rwkv_r_k, rwkv_lnx_w, rwkv_lnx_b, w_out_a, w_out_b, w_out, ffn_norm_g, router_w, router_b, exp_w1, exp_b1, exp_w2, exp_b2, ple_norm_g, ple_gate_w, ple_proj_w, final_norm_g):
    b, s, d = x.shape
    n = b * s
    depth = w_in.shape[0]
    row = lambda t: t.reshape(1, -1)
    x2 = x.reshape(n, d)
    for i in range(depth):
        c_sb = 3 * SB_WIDTH
        c_rw = c_sb + RWKV_COLS
        w = w_in[i]
        wqkv = jnp.concatenate([w[:, :SB_WIDTH] * HEAD_DIM ** -0.5, w[:, SB_WIDTH:c_sb]], axis=1).astype(BF16)
        qkv, zr, zg = _inproj(x2, row(mix_norm_g[i]), wqkv, w[:, c_sb:c_rw].astype(BF16), w[:, c_rw:].astype(BF16))
        ya = _sbattn(qkv.reshape(b, s, c_sb))
        wup = _pad_rows(rwkv_w_up[i], 0, LANES)
        aup = _pad_rows(rwkv_a_up[i], DECAY_RANK, LANES)
        r, kh, v, kkn, a, lw, g = _rwprep(
            zr.reshape(b, s, RWKV_COLS), row(rwkv_mu[i]), row(rwkv_w0[i]), wup, row(rwkv_a0[i]), aup,
            rwkv_g_up[i], row(rwkv_k_k[i]), row(rwkv_k_a[i]))
        yb = _rwscan(r, kh, v, kkn, a, lw, g, row(rwkv_lnx_w[i]), row(rwkv_lnx_b[i]), row(rwkv_r_k[i]))
        x1, h2, top_idx, gate = _mixout(
            x2, ya.reshape(n, SB_WIDTH), yb.reshape(n, RWKV_WIDTH), zg,
            w_out_a[i].astype(BF16), w_out_b[i].astype(BF16), w_out[i].astype(BF16), row(ffn_norm_g[i]),
            router_w[i].T, router_b[i].reshape(N_EXPERTS, 1))
        block_e, nvalid, n_used, buf_tok, buf_dst, buf_w = _dispatch(top_idx.T, gate.T)
        w1 = exp_w1[i]
        b1 = exp_b1[i]
        ycomb = _moe(block_e, nvalid, n_used, buf_tok, buf_dst, buf_w, h2,
                     w1[:, :, 0::2].astype(BF16), w1[:, :, 1::2].astype(BF16),
                     b1[:, None, 0::2], b1[:, None, 1::2], exp_w2[i].astype(BF16), exp_b2[i][:, None, :],
                     n * TOP_K)
        x2 = _final(x1, ycomb, p[i].reshape(n, PLE_DIM), row(ple_norm_g[i]), ple_gate_w[i].astype(BF16),
                    ple_proj_w[i].astype(BF16), row(final_norm_g), i == depth - 1)
    return x2.reshape(b, s, d)
```

```python
import functools

import jax
import jax.numpy as jnp
from jax import lax
from jax.experimental import pallas as pl
from jax.experimental.pallas import tpu as pltpu

F32 = jnp.float32
BF16 = jnp.bfloat16

D_MODEL = 1024
HEAD_DIM = 64
SB_WIDTH = 512
RWKV_WIDTH = 512
DECAY_RANK = 64
ICLR_RANK = 64
GATE_RANK = 128
RWKV_COLS = 3 * RWKV_WIDTH + DECAY_RANK + ICLR_RANK + GATE_RANK
PLE_DIM = 256
N_EXPERTS = 32
TOP_K = 4
EXPERT_HIDDEN = D_MODEL
SWIGLU_LIMIT = 7.0
SWIGLU_ALPHA = 1.702
MOE_BLOCK = 256
RMS_EPS = 1e-5
GN_EPS = 64e-5

LANES = 128
ROW_TILE = 256
ATTN_BLOCK = 256
ATTN_HALVES = 2
CHUNK = 64
VMEM_LIMIT = 56 * 1024 * 1024


def _params(sem, vmem=VMEM_LIMIT):
    return pltpu.CompilerParams(dimension_semantics=sem, vmem_limit_bytes=vmem)


def _dot(a, b):
    return jnp.dot(a.astype(BF16), b.astype(BF16), preferred_element_type=F32)


def _dot_nt(a, b):
    return lax.dot_general(a.astype(BF16), b.astype(BF16), (((1,), (1,)), ((), ())),
                           preferred_element_type=F32)


def _split(a):
    hi = a.astype(BF16)
    lo = (a - hi.astype(F32)).astype(BF16)
    return hi, lo


def _dot_exact_lhs(mask_bf16, a):
    hi, lo = _split(a)
    return (jnp.dot(mask_bf16, hi, preferred_element_type=F32)
            + jnp.dot(mask_bf16, lo, preferred_element_type=F32))


def _dot_exact_rhs(a, mask_bf16):
    hi, lo = _split(a)
    return (jnp.dot(hi, mask_bf16, preferred_element_type=F32)
            + jnp.dot(lo, mask_bf16, preferred_element_type=F32))


def _softplus(z):
    return jnp.maximum(z, 0.0) + jnp.log(1.0 + jnp.exp(-jnp.abs(z)))


def _sigmoid(z):
    return 1.0 / (1.0 + jnp.exp(-z))


def _rms(x, g):
    ms = jnp.mean(x * x, axis=-1, keepdims=True)
    return x * lax.rsqrt(ms + RMS_EPS) * g


def _inproj_kernel(x_ref, g_ref, wqkv_ref, wr_ref, wg_ref, qkv_ref, zr_ref, zg_ref):
    h = _rms(x_ref[...], g_ref[...]).astype(BF16)
    qkv_ref[...] = jnp.dot(h, wqkv_ref[...], preferred_element_type=F32).astype(BF16)
    zr_ref[...] = jnp.dot(h, wr_ref[...], preferred_element_type=F32)
    zg_ref[...] = jnp.dot(h, wg_ref[...], preferred_element_type=F32)


def _inproj(x2, g, wqkv, wr, wg):
    n = x2.shape[0]
    tm = ROW_TILE
    full = lambda a: pl.BlockSpec(a.shape, lambda i: (0,) * a.ndim)
    rows = lambda c: pl.BlockSpec((tm, c), lambda i: (i, 0))
    return pl.pallas_call(
        _inproj_kernel,
        grid=(n // tm,),
        in_specs=[rows(D_MODEL), full(g), full(wqkv), full(wr), full(wg)],
        out_specs=[rows(3 * SB_WIDTH), rows(RWKV_COLS), rows(2 * D_MODEL)],
        out_shape=[jax.ShapeDtypeStruct((n, 3 * SB_WIDTH), BF16),
                   jax.ShapeDtypeStruct((n, RWKV_COLS), F32),
                   jax.ShapeDtypeStruct((n, 2 * D_MODEL), F32)],
        compiler_params=_params(("parallel",)),
        name="inproj",
    )(x2, g, wqkv, wr, wg)


def _sbattn_kernel(q_ref, k_ref, v_ref, o_ref):
    tb = ATTN_BLOCK
    i = pl.program_id(2)
    lane = lax.broadcasted_iota(jnp.int32, (tb, LANES), 1)
    row = lax.broadcasted_iota(jnp.int32, (tb, tb), 0)
    col = lax.broadcasted_iota(jnp.int32, (tb, tb), 1)
    causal = col < row
    later = jnp.where(row > col, 1.0, 0.0).astype(BF16)

    def kv(j):
        at = pl.ds(pl.multiple_of(j * tb, tb), tb)
        return k_ref[0, at, :], v_ref[0, at, :]

    def block(qh, kb, vb, state, diag):
        carry, acc = state
        z = lax.dot_general(qh, kb, (((1,), (1,)), ((), ())), preferred_element_type=F32)
        sp = _softplus(z)
        l1 = -sp
        if diag:
            l1 = jnp.where(causal, l1, 0.0)
        after = _dot_exact_rhs(l1, later)
        att = jnp.exp((z - sp) + after + carry)
        if diag:
            att = jnp.where(causal, att, 0.0)
        acc = acc + jnp.dot(att.astype(BF16), vb, preferred_element_type=F32)
        carry = carry + jnp.sum(l1, axis=1, keepdims=True)
        return carry, acc

    zero = (jnp.zeros((tb, 1), F32), jnp.zeros((tb, LANES), F32))
    qs, states = [], []
    for half in range(ATTN_HALVES):
        q = q_ref[0, half * tb:(half + 1) * tb, :]
        kb, vb = kv(ATTN_HALVES * i + half)
        for head in range(2):
            in_head = (lane < HEAD_DIM) if head == 0 else (lane >= HEAD_DIM)
            qh = jnp.where(in_head, q, jnp.zeros_like(q))
            qs.append(qh)
            states.append(block(qh, kb, vb, zero, True))
    kb, vb = kv(ATTN_HALVES * i)
    for c in (2, 3):
        states[c] = block(qs[c], kb, vb, states[c], False)

    def body(t, st):
        kb, vb = kv(ATTN_HALVES * i - 1 - t)
        return tuple(block(qs[c], kb, vb, st[c], False) for c in range(4))

    states = lax.fori_loop(0, ATTN_HALVES * i, body, tuple(states))
    for half in range(ATTN_HALVES):
        o_ref[0, half * tb:(half + 1) * tb, :] = jnp.where(
            lane < HEAD_DIM, states[2 * half][1], states[2 * half + 1][1]).astype(BF16)


def _sbattn(qkv3):
    b, s, _ = qkv3.shape
    tb = ATTN_BLOCK * ATTN_HALVES
    pairs = SB_WIDTH // LANES
    return pl.pallas_call(
        _sbattn_kernel,
        grid=(b, pairs, s // tb),
        in_specs=[pl.BlockSpec((1, tb, LANES), lambda bb, p, i: (bb, i, p)),
                  pl.BlockSpec((1, s, LANES), lambda bb, p, i: (bb, 0, pairs + p)),
                  pl.BlockSpec((1, s, LANES), lambda bb, p, i: (bb, 0, 2 * pairs + p))],
        out_specs=pl.BlockSpec((1, tb, LANES), lambda bb, p, i: (bb, i, p)),
        out_shape=jax.ShapeDtypeStruct((b, s, SB_WIDTH), BF16),
        compiler_params=_params(("parallel", "parallel", "arbitrary")),
        name="sbattn",
    )(qkv3, qkv3, qkv3)


def _head_sums(t):
    lane = lax.broadcasted_iota(jnp.int32, t.shape, 1)
    low = lane < HEAD_DIM
    s_low = jnp.sum(jnp.where(low, t, 0.0), axis=1, keepdims=True)
    s_high = jnp.sum(jnp.where(low, 0.0, t), axis=1, keepdims=True)
    return jnp.where(low, s_low, s_high)


def _rwprep_kernel(z_ref, zp_ref, mu_ref, w0_ref, wup_ref, a0_ref, aup_ref, gup_ref, kk_ref, ka_ref,
                   r_ref, k_ref, v_ref, kkn_ref, a_ref, lw_ref, g_ref):
    ts = z_ref.shape[1]
    i = pl.program_id(1)
    z = z_ref[0]
    rowid = lax.broadcasted_iota(jnp.int32, z.shape, 0)
    last = jnp.where(i == 0, 0.0, zp_ref[0, 7:8, :])
    prev = jnp.where(rowid == 0, last, pltpu.roll(z, 1, 0))
    zs = z + mu_ref[...] * (prev - z)
    c1, c2, c3 = RWKV_WIDTH, 2 * RWKV_WIDTH, 3 * RWKV_WIDTH
    r, k, v = zs[:, :c1], zs[:, c1:c2], zs[:, c2:c3]
    xwa = zs[:, c3:c3 + LANES]
    xg = zs[:, c3 + LANES:]
    hp = lax.Precision.HIGHEST
    w_pre = w0_ref[...] + jnp.dot(jnp.tanh(xwa), wup_ref[...], precision=hp, preferred_element_type=F32)
    w_log = -_softplus(-w_pre) - 0.5
    lw = -jnp.exp(w_log)
    a = _sigmoid(a0_ref[...] + jnp.dot(xwa, aup_ref[...], precision=hp, preferred_element_type=F32))
    g = jnp.dot(_sigmoid(xg), gup_ref[...], precision=hp, preferred_element_type=F32)
    kk = k * kk_ref[...]
    for t in range(RWKV_WIDTH // LANES):
        sl = slice(t * LANES, (t + 1) * LANES)
        kt = kk[:, sl]
        nrm = jnp.sqrt(_head_sums(kt * kt))
        kkn_ref[0, :, sl] = kt / jnp.maximum(nrm, 1e-12)
    r_ref[0] = r
    k_ref[0] = k * (1.0 + (a - 1.0) * ka_ref[...])
    v_ref[0] = v
    a_ref[0] = a
    lw_ref[0] = lw
    g_ref[0] = g


def _rwprep(zr3, mu, w0, wup, a0, aup, gup, k_k, k_a):
    b, s, _ = zr3.shape
    ts = ROW_TILE
    full = lambda a: pl.BlockSpec(a.shape, lambda bb, i: (0,) * a.ndim)
    out = pl.BlockSpec((1, ts, RWKV_WIDTH), lambda bb, i: (bb, i, 0))
    return pl.pallas_call(
        _rwprep_kernel,
        grid=(b, s // ts),
        in_specs=[pl.BlockSpec((1, ts, RWKV_COLS), lambda bb, i: (bb, i, 0)),
                  pl.BlockSpec((1, 8, RWKV_COLS), lambda bb, i: (bb, jnp.maximum(i * (ts // 8) - 1, 0), 0)),
                  full(mu), full(w0), full(wup), full(a0), full(aup), full(gup), full(k_k), full(k_a)],
        out_specs=[out] * 7,
        out_shape=[jax.ShapeDtypeStruct((b, s, RWKV_WIDTH), F32)] * 7,
        compiler_params=_params(("parallel", "parallel")),
        name="rwprep",
    )(zr3, zr3, mu, w0, wup, a0, aup, gup, k_k, k_a)


def _unit_lower_inverse(lb):
    n = lb.shape[0]
    eye = jnp.where(lax.broadcasted_iota(jnp.int32, (n, n), 0) == lax.broadcasted_iota(jnp.int32, (n, n), 1),
                    1.0, 0.0)
    p = -lb
    t = eye + p
    steps = n.bit_length() - 1
    for _ in range(steps - 1):
        p = _dot(p, p)
        t = t + _dot(t, p)
    return t


def _rwscan_kernel(r_ref, k_ref, v_ref, kk_ref, a_ref, lw_ref, g_ref, lnw_ref, lnb_ref, rk_ref, o_ref, state_ref):
    c = CHUNK
    tile = r_ref.shape[1]
    pairs = RWKV_WIDTH // LANES

    @pl.when(pl.program_id(1) == 0)
    def _():
        state_ref[...] = jnp.zeros_like(state_ref)

    lane = lax.broadcasted_iota(jnp.int32, (c, LANES), 1)
    heads = (lane < HEAD_DIM, lane >= HEAD_DIM)
    trow = lax.broadcasted_iota(jnp.int32, (c, c), 0)
    tcol = lax.broadcasted_iota(jnp.int32, (c, c), 1)
    strict = tcol < trow
    incl = tcol <= trow
    cum = jnp.where(incl, 1.0, 0.0).astype(BF16)
    srow = lax.broadcasted_iota(jnp.int32, (LANES, LANES), 0) < HEAD_DIM
    scol = lax.broadcasted_iota(jnp.int32, (LANES, LANES), 1) < HEAD_DIM
    same_head = srow == scol
    def pair_chunk(p, sl):
        cols = slice(p * LANES, (p + 1) * LANES)
        lnw, lnb, rk = lnw_ref[:, cols], lnb_ref[:, cols], rk_ref[:, cols]
        state = state_ref[p]
        r, k, v = r_ref[0, sl, cols], k_ref[0, sl, cols], v_ref[0, sl, cols]
        kk, a, lw = kk_ref[0, sl, cols], a_ref[0, sl, cols], lw_ref[0, sl, cols]
        lg = _dot_exact_lhs(cum, lw)
        lg_end = lg[c - 1:c, :]
        e_in = jnp.exp(lg)
        e_ex = jnp.exp(lg - lw)
        e_inv = jnp.exp(-lg)
        e_end = jnp.exp(lg_end - lg)
        ab = kk * a
        kk_t = kk * e_ex
        r_t = r * e_in
        b_t = ab * e_inv
        k_t = k * e_inv
        b_end = ab * e_end
        k_end = k * e_end
        w_all = jnp.zeros((c, LANES), F32)
        uv_all = jnp.zeros((c, LANES), F32)
        yv_all = jnp.zeros((c, LANES), F32)
        arb = []
        for hm in heads:
            kk_h = jnp.where(hm, kk_t, 0.0)
            r_h = jnp.where(hm, r_t, 0.0)
            v_h = jnp.where(hm, v, 0.0)
            l_b = jnp.where(strict, _dot_nt(kk_h, b_t), 0.0)
            l_k = jnp.where(strict, _dot_nt(kk_h, k_t), 0.0)
            a_rb = jnp.where(incl, _dot_nt(r_h, b_t), 0.0)
            a_rk = jnp.where(incl, _dot_nt(r_h, k_t), 0.0)
            t_inv = _unit_lower_inverse(l_b)
            w_all = w_all + _dot(t_inv, kk_h)
            uv_all = uv_all + _dot(t_inv, _dot(l_k, v_h))
            yv_all = yv_all + _dot(a_rk, v_h)
            arb.append(a_rb)
        zs = _dot_nt(jnp.concatenate([w_all, r_t], axis=0), state)
        u = -(zs[:c] + uv_all)
        y = zs[c:] + yv_all
        for hm, a_rb in zip(heads, arb):
            y = y + _dot(a_rb, jnp.where(hm, u, 0.0))
        upd = _dot(jnp.concatenate([u, v], axis=0).T, jnp.concatenate([b_end, k_end], axis=0))
        state_ref[p] = state * jnp.exp(lg_end) + jnp.where(same_head, upd, 0.0)
        mean = _head_sums(y) * (1.0 / HEAD_DIM)
        d = y - mean
        var = _head_sums(d * d) * (1.0 / HEAD_DIM)
        yn = d * lax.rsqrt(var + GN_EPS) * lnw + lnb
        bonus = _head_sums(r * k * rk) * v
        o_ref[0, sl, cols] = ((yn + bonus) * g_ref[0, sl, cols]).astype(o_ref.dtype)

    def chunk(ci, _):
        sl = pl.ds(pl.multiple_of(ci * c, c), c)
        for p in range(pairs):
            pair_chunk(p, sl)
        return 0

    lax.fori_loop(0, tile // c, chunk, 0)


def _rwscan(r, k, v, kk, a, lw, g, lnw, lnb, rk):
    b, s, _ = r.shape
    ts = ROW_TILE
    pairs = RWKV_WIDTH // LANES
    seq = pl.BlockSpec((1, ts, RWKV_WIDTH), lambda bb, i: (bb, i, 0))
    vec = pl.BlockSpec((1, RWKV_WIDTH), lambda bb, i: (0, 0))
    return pl.pallas_call(
        _rwscan_kernel,
        grid=(b, s // ts),
        in_specs=[seq] * 7 + [vec] * 3,
        out_specs=seq,
        out_shape=jax.ShapeDtypeStruct((b, s, RWKV_WIDTH), BF16),
        scratch_shapes=[pltpu.VMEM((pairs, LANES, LANES), F32)],
        compiler_params=_params(("parallel", "arbitrary")),
        name="rwscan",
    )(r, k, v, kk, a, lw, g, lnw, lnb, rk)


def _mixout_kernel(x_ref, ya_ref, yb_ref, zg_ref, woa_ref, wob_ref, wo_ref, fg_ref, rwt_ref, rb_ref,
                   x1_ref, h2_ref, idx_ref, gate_ref):
    ga = _sigmoid(zg_ref[:, :D_MODEL])
    gb = _sigmoid(zg_ref[:, D_MODEL:])
    merged = (ga * jnp.dot(ya_ref[...], woa_ref[...], preferred_element_type=F32)
              + gb * jnp.dot(yb_ref[...], wob_ref[...], preferred_element_type=F32))
    x1 = x_ref[...] + jnp.dot(merged.astype(BF16), wo_ref[...], preferred_element_type=F32)
    x1_ref[...] = x1
    h2 = _rms(x1, fg_ref[...])
    h2_ref[...] = h2
    logits = lax.dot_general(rwt_ref[...], h2, (((1,), (1,)), ((), ())),
                             precision=lax.Precision.HIGHEST, preferred_element_type=F32) + rb_ref[...]
    eid = lax.broadcasted_iota(jnp.int32, logits.shape, 0).astype(F32)
    vals, ids = [], []
    for _ in range(TOP_K):
        m = jnp.max(logits, axis=0, keepdims=True)
        sel = jnp.min(jnp.where(logits == m, eid, float(N_EXPERTS)), axis=0, keepdims=True)
        vals.append(m)
        ids.append(sel.astype(jnp.int32))
        logits = jnp.where(eid == sel, -jnp.inf, logits)
    ex = [jnp.exp(vv - vals[0]) for vv in vals]
    den = ex[0] + ex[1] + ex[2] + ex[3]
    idx_ref[...] = jnp.concatenate(ids, axis=0)
    gate_ref[...] = jnp.concatenate([e / den for e in ex], axis=0)


def _mixout(x2, ya, yb, zg, woa, wob, wo, fg, rwt, rb):
    n = x2.shape[0]
    tm = ROW_TILE
    full = lambda a: pl.BlockSpec(a.shape, lambda i: (0,) * a.ndim)
    rows = lambda c: pl.BlockSpec((tm, c), lambda i: (i, 0))
    cols = pl.BlockSpec((TOP_K, tm), lambda i: (0, i))
    return pl.pallas_call(
        _mixout_kernel,
        grid=(n // tm,),
        in_specs=[rows(D_MODEL), rows(SB_WIDTH), rows(RWKV_WIDTH), rows(2 * D_MODEL),
                  full(woa), full(wob), full(wo), full(fg), full(rwt), full(rb)],
        out_specs=[rows(D_MODEL), rows(D_MODEL), cols, cols],
        out_shape=[jax.ShapeDtypeStruct((n, D_MODEL), F32),
                   jax.ShapeDtypeStruct((n, D_MODEL), F32),
                   jax.ShapeDtypeStruct((TOP_K, n), jnp.int32),
                   jax.ShapeDtypeStruct((TOP_K, n), F32)],
        compiler_params=_params(("parallel",)),
        name="mixout",
    )(x2, ya, yb, zg, woa, wob, wo, fg, rwt, rb)


def _split_gate_up_kernel(w_ref, g_ref, u_ref):
    width = 2 * LANES
    rows = lax.broadcasted_iota(jnp.int32, (width, width), 0)
    cols = lax.broadcasted_iota(jnp.int32, (width, width), 1)
    src = jnp.where(cols < LANES, 2 * cols, 2 * (cols - LANES) + 1)
    sel = jnp.where(rows == src, 1.0, 0.0).astype(BF16)
    for c in range(w_ref.shape[2] // width):
        blk = w_ref[0, :, c * width:(c + 1) * width].astype(BF16)
        out = jnp.dot(blk, sel, preferred_element_type=F32)
        g_ref[0, :, c * LANES:(c + 1) * LANES] = out[:, :LANES].astype(BF16)
        u_ref[0, :, c * LANES:(c + 1) * LANES] = out[:, LANES:].astype(BF16)


def _split_gate_up(w1):
    e, d, f2 = w1.shape
    tr = 512
    out = pl.BlockSpec((1, tr, f2 // 2), lambda ee, i: (ee, i, 0))
    return pl.pallas_call(
        _split_gate_up_kernel,
        grid=(e, d // tr),
        in_specs=[pl.BlockSpec((1, tr, f2), lambda ee, i: (ee, i, 0))],
        out_specs=[out, out],
        out_shape=[jax.ShapeDtypeStruct((e, d, f2 // 2), BF16)] * 2,
        compiler_params=_params(("parallel", "parallel")),
        name="split_gate_up",
    )(w1)


def _moe_kernel(be_ref, nused_ref, tok0_ref, tokn_ref, dst_ref, wrow_ref, h_hbm,
                w1g_ref, w1u_ref, b1g_ref, b1u_ref, w2_ref, b2_ref, y_hbm,
                xbuf, ybuf, gsem, ssem):
    g = MOE_BLOCK
    i = pl.program_id(0)
    n_used = nused_ref[0]
    slot = lax.rem(i, 2)
    n_rows = y_hbm.shape[0] - g

    def gather(tok_ref, s):
        for r in range(g):
            pltpu.make_async_copy(h_hbm.at[pl.ds(tok_ref[0, 0, r], 1)], xbuf.at[s, pl.ds(r, 1)],
                                  gsem.at[s]).start(priority=0)

    def slot_wait(buf, sem):
        pltpu.make_async_copy(buf, buf, sem).wait()

    @pl.when(i == 0)
    def _():
        ybuf[0] = jnp.zeros((g, D_MODEL), F32)
        spare = pltpu.make_async_copy(ybuf.at[0], y_hbm.at[pl.ds(n_rows, g)], ssem)
        spare.start()
        spare.wait()

    @pl.when(jnp.logical_and(i == 0, n_used > 0))
    def _():
        gather(tok0_ref, 0)

    @pl.when(i + 1 < n_used)
    def _():
        gather(tokn_ref, 1 - slot)

    @pl.when(i < n_used)
    def _():
        slot_wait(xbuf.at[slot], gsem.at[slot])
        xb = xbuf[slot].astype(BF16)
        g_lin = jnp.dot(xb, w1g_ref[0], preferred_element_type=F32) + b1g_ref[0]
        u_lin = jnp.dot(xb, w1u_ref[0], preferred_element_type=F32) + b1u_ref[0]
        g_lin = jnp.minimum(g_lin, SWIGLU_LIMIT)
        u_lin = jnp.clip(u_lin, -SWIGLU_LIMIT, SWIGLU_LIMIT)
        glu = g_lin * _sigmoid(SWIGLU_ALPHA * g_lin)
        hmid = ((u_lin + 1.0) * glu).astype(BF16)
        y = jnp.dot(hmid, w2_ref[0], preferred_element_type=F32) + b2_ref[0]
        rr = lax.broadcasted_iota(jnp.int32, (g, g), 0)
        cc = lax.broadcasted_iota(jnp.int32, (g, g), 1)
        wdiag = jnp.where(rr == cc, jnp.broadcast_to(wrow_ref[0], (g, g)), 0.0)
        wcol = jnp.dot(wdiag, jnp.ones((g, LANES), F32), precision=lax.Precision.HIGHEST,
                       preferred_element_type=F32)
        ybuf[slot] = y * jnp.tile(wcol, (1, D_MODEL // LANES))

        @pl.when(i > 0)
        def _():
            slot_wait(ybuf.at[1 - slot], ssem)

        for r in range(g):
            pltpu.make_async_copy(ybuf.at[slot, pl.ds(r, 1)], y_hbm.at[pl.ds(dst_ref[0, 0, r], 1)],
                                  ssem).start(priority=1)

        @pl.when(i == n_used - 1)
        def _():
            slot_wait(ybuf.at[slot], ssem)


def _moe(block_e, n_used, tok, dst, wrow, h2, w1g, w1u, b1g, b1u, w2, b2, n_assign):
    nb = tok.shape[0]
    g = MOE_BLOCK
    d = D_MODEL
    f = EXPERT_HIDDEN
    smem = lambda im: pl.BlockSpec((1, 1, g), im, memory_space=pltpu.SMEM)
    ew = lambda shape: pl.BlockSpec((1,) + shape, lambda i, be, nu: (be[i], 0, 0))
    grid_spec = pltpu.PrefetchScalarGridSpec(
        num_scalar_prefetch=2,
        grid=(nb,),
        in_specs=[smem(lambda i, be, nu: (0, 0, 0)),
                  smem(lambda i, be, nu: (jnp.minimum(i + 1, nb - 1), 0, 0)),
                  smem(lambda i, be, nu: (i, 0, 0)),
                  pl.BlockSpec((1, 1, g), lambda i, be, nu: (i, 0, 0)),
                  pl.BlockSpec(memory_space=pl.ANY),
                  ew((d, f)), ew((d, f)), ew((1, f)), ew((1, f)), ew((f, d)), ew((1, d))],
        out_specs=pl.BlockSpec(memory_space=pl.ANY),
        scratch_shapes=[pltpu.VMEM((2, g, d), F32), pltpu.VMEM((2, g, d), F32),
                        pltpu.SemaphoreType.DMA((2,)), pltpu.SemaphoreType.DMA(())],
    )
    return pl.pallas_call(
        _moe_kernel,
        grid_spec=grid_spec,
        out_shape=jax.ShapeDtypeStruct((n_assign + g, d), F32),
        compiler_params=_params(("arbitrary",)),
        name="moe",
    )(block_e, n_used, tok, tok, dst, wrow, h2, w1g, w1u, b1g, b1u, w2, b2)


def _final_kernel(x1_ref, y0_ref, y1_ref, y2_ref, y3_ref, p_ref, pg_ref, wgate_ref, wproj_ref, fg_ref, o_ref,
                  *, last):
    x2 = x1_ref[...] + y0_ref[...] + y1_ref[...] + y2_ref[...] + y3_ref[...]
    hp = _rms(x2, pg_ref[...]).astype(BF16)
    gate = _sigmoid(jnp.dot(hp, wgate_ref[...], preferred_element_type=F32))
    proj = jnp.dot(p_ref[...].astype(BF16), wproj_ref[...], preferred_element_type=F32)
    x3 = x2 + gate * proj
    o_ref[...] = _rms(x3, fg_ref[...]) if last else x3


def _final(x1, ycomb, p2, pg, wgate, wproj, fg, last):
    n = x1.shape[0]
    tm = ROW_TILE
    full = lambda a: pl.BlockSpec(a.shape, lambda i: (0,) * a.ndim)
    rows = lambda c: pl.BlockSpec((tm, c), lambda i: (i, 0))
    ysp = lambda k: pl.BlockSpec((tm, D_MODEL), lambda i: (k * (n // tm) + i, 0))
    return pl.pallas_call(
        functools.partial(_final_kernel, last=last),
        grid=(n // tm,),
        in_specs=[rows(D_MODEL), ysp(0), ysp(1), ysp(2), ysp(3), rows(PLE_DIM),
                  full(pg), full(wgate), full(wproj), full(fg)],
        out_specs=rows(D_MODEL),
        out_shape=jax.ShapeDtypeStruct((n, D_MODEL), F32),
        compiler_params=_params(("parallel",)),
        name="final",
    )(x1, ycomb, ycomb, ycomb, ycomb, p2, pg, wgate, wproj, fg)


def _dispatch(top_idx, gate):
    n = top_idx.shape[1]
    nk = n * TOP_K
    i32 = jnp.int32
    flat_e = top_idx.reshape(-1)
    order = jnp.argsort(flat_e).astype(i32)
    experts = jnp.arange(N_EXPERTS, dtype=i32)
    counts = jnp.sum((flat_e[:, None] == experts[None, :]).astype(i32), axis=0)
    offsets = jnp.cumsum(counts) - counts
    padded = (counts + MOE_BLOCK - 1) // MOE_BLOCK * MOE_BLOCK
    pad_end = jnp.cumsum(padded)
    pad_off = pad_end - padded
    n_blocks = -(-nk // MOE_BLOCK) + N_EXPERTS
    block_start = jnp.arange(n_blocks, dtype=i32) * MOE_BLOCK
    block_e = jnp.minimum(jnp.sum((pad_end[None, :] <= block_start[:, None]).astype(i32), axis=1), N_EXPERTS - 1)
    within = jnp.arange(MOE_BLOCK, dtype=i32)[None, :]
    rank = (block_start - pad_off[block_e])[:, None] + within
    valid = rank < counts[block_e][:, None]
    assign = order[jnp.clip(offsets[block_e][:, None] + rank, 0, nk - 1)]
    buf_tok = jnp.where(valid, assign % n, 0)
    buf_dst = jnp.where(valid, assign, nk + within)
    buf_w = jnp.where(valid, gate.reshape(-1)[assign], 0.0)
    n_used = (pad_end[-1] // MOE_BLOCK).astype(i32).reshape(1)
    shape = (n_blocks, 1, MOE_BLOCK)
    return block_e, n_used, buf_tok.reshape(shape), buf_dst.reshape(shape), buf_w.reshape(shape)


def _pad_rows(w, before, total):
    return jnp.zeros((total, w.shape[1]), w.dtype).at[before:before + w.shape[0]].set(w)


def kernel(x, p, mix_norm_g, w_in, rwkv_mu, rwkv_w0, rwkv_w_up, rwkv_a0, rwkv_a_up, rwkv_g_up, rwkv_k_k, rwkv_k_a, rwkv_r_k, rwkv_lnx_w, rwkv_lnx_b, w_out_a, w_out_b, w_out, ffn_norm_g, router_w, router_b, exp_w1, exp_b1, exp_w2, exp_b2, ple_norm_g, ple_gate_w, ple_proj_w, final_norm_g):
    b, s, d = x.shape
    n = b * s
    depth = w_in.shape[0]
    row = lambda t: t.reshape(1, -1)
    x2 = x.reshape(n, d)
    for i in range(depth):
        c_sb = 3 * SB_WIDTH
        c_rw = c_sb + RWKV_COLS
        w = w_in[i]
        wqkv = jnp.concatenate([w[:, :SB_WIDTH] * HEAD_DIM ** -0.5, w[:, SB_WIDTH:c_sb]], axis=1).astype(BF16)
        qkv, zr, zg = _inproj(x2, row(mix_norm_g[i]), wqkv, w[:, c_sb:c_rw].astype(BF16), w[:, c_rw:].astype(BF16))
        ya = _sbattn(qkv.reshape(b, s, c_sb))
        wup = _pad_rows(rwkv_w_up[i], 0, LANES)
        aup = _pad_rows(rwkv_a_up[i], DECAY_RANK, LANES)
        r, kh, v, kkn, a, lw, g = _rwprep(
            zr.reshape(b, s, RWKV_COLS), row(rwkv_mu[i]), row(rwkv_w0[i]), wup, row(rwkv_a0[i]), aup,
            rwkv_g_up[i], row(rwkv_k_k[i]), row(rwkv_k_a[i]))
        yb = _rwscan(r, kh, v, kkn, a, lw, g, row(rwkv_lnx_w[i]), row(rwkv_lnx_b[i]), row(rwkv_r_k[i]))
        x1, h2, top_idx, gate = _mixout(
            x2, ya.reshape(n, SB_WIDTH), yb.reshape(n, RWKV_WIDTH), zg,
            w_out_a[i].astype(BF16), w_out_b[i].astype(BF16), w_out[i].astype(BF16), row(ffn_norm_g[i]),
            router_w[i].T, router_b[i].reshape(N_EXPERTS, 1))
        block_e, n_used, buf_tok, buf_dst, buf_w = _dispatch(top_idx, gate)
        w1g, w1u = _split_gate_up(exp_w1[i])
        b1 = exp_b1[i]
        ycomb = _moe(block_e, n_used, buf_tok, buf_dst, buf_w, h2, w1g, w1u,
                     b1[:, None, 0::2], b1[:, None, 1::2], exp_w2[i].astype(BF16), exp_b2[i][:, None, :],
                     n * TOP_K)
        x2 = _final(x1, ycomb, p[i].reshape(n, PLE_DIM), row(ple_norm_g[i]), ple_gate_w[i].astype(BF16),
                    ple_proj_w[i].astype(BF16), row(final_norm_g), i == depth - 1)
    return x2.reshape(b, s, d)
```

```python
import functools

import jax
import jax.numpy as jnp
from jax import lax
from jax.experimental import pallas as pl
from jax.experimental.pallas import tpu as pltpu

F32 = jnp.float32
BF16 = jnp.bfloat16

D_MODEL = 1024
HEAD_DIM = 64
SB_WIDTH = 512
RWKV_WIDTH = 512
DECAY_RANK = 64
ICLR_RANK = 64
GATE_RANK = 128
RWKV_COLS = 3 * RWKV_WIDTH + DECAY_RANK + ICLR_RANK + GATE_RANK
PLE_DIM = 256
N_EXPERTS = 32
TOP_K = 4
EXPERT_HIDDEN = D_MODEL
SWIGLU_LIMIT = 7.0
SWIGLU_ALPHA = 1.702
MOE_BLOCK = 256
RMS_EPS = 1e-5
GN_EPS = 64e-5

LANES = 128
TOKEN_TILE_ROWS = D_MODEL // LANES
ROW_TILE = 256
ATTN_BLOCK = 256
ATTN_HALVES = 2
CHUNK = 64
VMEM_LIMIT = 56 * 1024 * 1024


def _params(sem, vmem=VMEM_LIMIT):
    return pltpu.CompilerParams(dimension_semantics=sem, vmem_limit_bytes=vmem)


def _dot(a, b):
    return jnp.dot(a.astype(BF16), b.astype(BF16), preferred_element_type=F32)


def _dot_nt(a, b):
    return lax.dot_general(a.astype(BF16), b.astype(BF16), (((1,), (1,)), ((), ())),
                           preferred_element_type=F32)


def _split(a):
    hi = a.astype(BF16)
    lo = (a - hi.astype(F32)).astype(BF16)
    return hi, lo


def _dot_exact_lhs(mask_bf16, a):
    hi, lo = _split(a)
    return (jnp.dot(mask_bf16, hi, preferred_element_type=F32)
            + jnp.dot(mask_bf16, lo, preferred_element_type=F32))


def _softplus(z):
    return jnp.maximum(z, 0.0) + jnp.log(1.0 + jnp.exp(-jnp.abs(z)))


def _sigmoid(z):
    return 1.0 / (1.0 + jnp.exp(-z))


def _rms(x, g):
    ms = jnp.mean(x * x, axis=-1, keepdims=True)
    return x * lax.rsqrt(ms + RMS_EPS) * g


def _inproj_kernel(x_ref, g_ref, wqkv_ref, wr_ref, wg_ref, qkv_ref, zr_ref, zg_ref):
    h = _rms(x_ref[...], g_ref[...]).astype(BF16)
    qkv_ref[...] = jnp.dot(h, wqkv_ref[...], preferred_element_type=F32).astype(BF16)
    zr_ref[...] = jnp.dot(h, wr_ref[...], preferred_element_type=F32)
    zg_ref[...] = jnp.dot(h, wg_ref[...], preferred_element_type=F32)


def _inproj(x2, g, wqkv, wr, wg):
    n = x2.shape[0]
    tm = ROW_TILE
    full = lambda a: pl.BlockSpec(a.shape, lambda i: (0,) * a.ndim)
    rows = lambda c: pl.BlockSpec((tm, c), lambda i: (i, 0))
    return pl.pallas_call(
        _inproj_kernel,
        grid=(n // tm,),
        in_specs=[rows(D_MODEL), full(g), full(wqkv), full(wr), full(wg)],
        out_specs=[rows(3 * SB_WIDTH), rows(RWKV_COLS), rows(2 * D_MODEL)],
        out_shape=[jax.ShapeDtypeStruct((n, 3 * SB_WIDTH), BF16),
                   jax.ShapeDtypeStruct((n, RWKV_COLS), F32),
                   jax.ShapeDtypeStruct((n, 2 * D_MODEL), F32)],
        compiler_params=_params(("parallel",)),
        name="inproj",
    )(x2, g, wqkv, wr, wg)


def _sbattn_kernel(q_ref, k_ref, v_ref, o_ref):
    tb = ATTN_BLOCK
    i = pl.program_id(2)
    lane = lax.broadcasted_iota(jnp.int32, (tb, LANES), 1)
    row = lax.broadcasted_iota(jnp.int32, (tb, tb), 0)
    col = lax.broadcasted_iota(jnp.int32, (tb, tb), 1)
    causal = col < row
    later = jnp.where(row > col, 1.0, 0.0).astype(BF16)

    def kv(j):
        at = pl.ds(pl.multiple_of(j * tb, tb), tb)
        return k_ref[0, at, :], v_ref[0, at, :]

    def blocks(qhs, kvs, states, diag):
        n = range(len(qhs))
        z = [lax.dot_general(qhs[c], kvs[c][0], (((1,), (1,)), ((), ())), preferred_element_type=F32) for c in n]
        sp = [_softplus(z[c]) for c in n]
        l1 = [jnp.where(causal, -sp[c], 0.0) if diag else -sp[c] for c in n]
        parts = [_split(l1[c]) for c in n]
        after = [jnp.dot(parts[c][0], later, preferred_element_type=F32)
                 + jnp.dot(parts[c][1], later, preferred_element_type=F32) for c in n]
        att = [jnp.exp((z[c] - sp[c]) + after[c] + states[c][0]) for c in n]
        if diag:
            att = [jnp.where(causal, att[c], 0.0) for c in n]
        acc = [states[c][1] + jnp.dot(att[c].astype(BF16), kvs[c][1], preferred_element_type=F32) for c in n]
        carry = [states[c][0] + jnp.sum(l1[c], axis=1, keepdims=True) for c in n]
        return [(carry[c], acc[c]) for c in n]

    zero = (jnp.zeros((tb, 1), F32), jnp.zeros((tb, LANES), F32))
    qs, kv_diag = [], []
    for half in range(ATTN_HALVES):
        q = q_ref[0, half * tb:(half + 1) * tb, :]
        for head in range(2):
            in_head = (lane < HEAD_DIM) if head == 0 else (lane >= HEAD_DIM)
            qs.append(jnp.where(in_head, q, jnp.zeros_like(q)))
            kv_diag.append(kv(ATTN_HALVES * i + half))
    states = blocks(qs, kv_diag, [zero] * 4, True)
    states[2:] = blocks(qs[2:], [kv(ATTN_HALVES * i)] * 2, states[2:], False)

    def body(t, st):
        return tuple(blocks(qs, [kv(ATTN_HALVES * i - 1 - t)] * 4, list(st), False))

    states = lax.fori_loop(0, ATTN_HALVES * i, body, tuple(states))
    for half in range(ATTN_HALVES):
        o_ref[0, half * tb:(half + 1) * tb, :] = jnp.where(
            lane < HEAD_DIM, states[2 * half][1], states[2 * half + 1][1]).astype(BF16)


def _sbattn(qkv3):
    b, s, _ = qkv3.shape
    tb = ATTN_BLOCK * ATTN_HALVES
    pairs = SB_WIDTH // LANES
    return pl.pallas_call(
        _sbattn_kernel,
        grid=(b, pairs, s // tb),
        in_specs=[pl.BlockSpec((1, tb, LANES), lambda bb, p, i: (bb, i, p)),
                  pl.BlockSpec((1, s, LANES), lambda bb, p, i: (bb, 0, pairs + p)),
                  pl.BlockSpec((1, s, LANES), lambda bb, p, i: (bb, 0, 2 * pairs + p))],
        out_specs=pl.BlockSpec((1, tb, LANES), lambda bb, p, i: (bb, i, p)),
        out_shape=jax.ShapeDtypeStruct((b, s, SB_WIDTH), BF16),
        compiler_params=_params(("parallel", "parallel", "arbitrary")),
        name="sbattn",
    )(qkv3, qkv3, qkv3)


def _head_sums(t):
    lane = lax.broadcasted_iota(jnp.int32, t.shape, 1)
    low = lane < HEAD_DIM
    s_low = jnp.sum(jnp.where(low, t, 0.0), axis=1, keepdims=True)
    s_high = jnp.sum(jnp.where(low, 0.0, t), axis=1, keepdims=True)
    return jnp.where(low, s_low, s_high)


def _rwprep_kernel(z_ref, zp_ref, mu_ref, w0_ref, wup_ref, a0_ref, aup_ref, gup_ref, kk_ref, ka_ref,
                   r_ref, k_ref, v_ref, kkn_ref, a_ref, lw_ref, g_ref):
    ts = z_ref.shape[1]
    i = pl.program_id(1)
    z = z_ref[0]
    rowid = lax.broadcasted_iota(jnp.int32, z.shape, 0)
    last = jnp.where(i == 0, 0.0, zp_ref[0, 7:8, :])
    prev = jnp.where(rowid == 0, last, pltpu.roll(z, 1, 0))
    zs = z + mu_ref[...] * (prev - z)
    c1, c2, c3 = RWKV_WIDTH, 2 * RWKV_WIDTH, 3 * RWKV_WIDTH
    r, k, v = zs[:, :c1], zs[:, c1:c2], zs[:, c2:c3]
    xwa = zs[:, c3:c3 + LANES]
    xg = zs[:, c3 + LANES:]
    hp = lax.Precision.HIGHEST
    w_pre = w0_ref[...] + jnp.dot(jnp.tanh(xwa), wup_ref[...], precision=hp, preferred_element_type=F32)
    w_log = -_softplus(-w_pre) - 0.5
    lw = -jnp.exp(w_log)
    a = _sigmoid(a0_ref[...] + jnp.dot(xwa, aup_ref[...], precision=hp, preferred_element_type=F32))
    g = jnp.dot(_sigmoid(xg), gup_ref[...], precision=hp, preferred_element_type=F32)
    kk = k * kk_ref[...]
    for t in range(RWKV_WIDTH // LANES):
        sl = slice(t * LANES, (t + 1) * LANES)
        kt = kk[:, sl]
        nrm = jnp.sqrt(_head_sums(kt * kt))
        kkn_ref[0, :, sl] = kt / jnp.maximum(nrm, 1e-12)
    r_ref[0] = r
    k_ref[0] = k * (1.0 + (a - 1.0) * ka_ref[...])
    v_ref[0] = v
    a_ref[0] = a
    lw_ref[0] = lw
    g_ref[0] = g


def _rwprep(zr3, mu, w0, wup, a0, aup, gup, k_k, k_a):
    b, s, _ = zr3.shape
    ts = ROW_TILE
    full = lambda a: pl.BlockSpec(a.shape, lambda bb, i: (0,) * a.ndim)
    out = pl.BlockSpec((1, ts, RWKV_WIDTH), lambda bb, i: (bb, i, 0))
    return pl.pallas_call(
        _rwprep_kernel,
        grid=(b, s // ts),
        in_specs=[pl.BlockSpec((1, ts, RWKV_COLS), lambda bb, i: (bb, i, 0)),
                  pl.BlockSpec((1, 8, RWKV_COLS), lambda bb, i: (bb, jnp.maximum(i * (ts // 8) - 1, 0), 0)),
                  full(mu), full(w0), full(wup), full(a0), full(aup), full(gup), full(k_k), full(k_a)],
        out_specs=[out] * 7,
        out_shape=[jax.ShapeDtypeStruct((b, s, RWKV_WIDTH), F32)] * 7,
        compiler_params=_params(("parallel", "parallel")),
        name="rwprep",
    )(zr3, zr3, mu, w0, wup, a0, aup, gup, k_k, k_a)


def _rwscan_kernel(r_ref, k_ref, v_ref, kk_ref, a_ref, lw_ref, g_ref, lnw_ref, lnb_ref, rk_ref, o_ref, state_ref):
    c = CHUNK
    tile = r_ref.shape[1]
    pairs = RWKV_WIDTH // LANES

    @pl.when(pl.program_id(1) == 0)
    def _():
        state_ref[...] = jnp.zeros_like(state_ref)

    lane = lax.broadcasted_iota(jnp.int32, (c, LANES), 1)
    heads = (lane < HEAD_DIM, lane >= HEAD_DIM)
    trow = lax.broadcasted_iota(jnp.int32, (c, c), 0)
    tcol = lax.broadcasted_iota(jnp.int32, (c, c), 1)
    strict = tcol < trow
    incl = tcol <= trow
    cum = jnp.where(incl, 1.0, 0.0).astype(BF16)
    srow = lax.broadcasted_iota(jnp.int32, (LANES, LANES), 0) < HEAD_DIM
    scol = lax.broadcasted_iota(jnp.int32, (LANES, LANES), 1) < HEAD_DIM
    same_head = srow == scol
    eye = jnp.where(trow == tcol, 1.0, 0.0)
    prange = range(pairs)
    chains = [(p, hm) for p in prange for hm in heads]

    def chunk(ci, _):
        sl = pl.ds(pl.multiple_of(ci * c, c), c)
        col = lambda p: slice(p * LANES, (p + 1) * LANES)
        r = [r_ref[0, sl, col(p)] for p in prange]
        k = [k_ref[0, sl, col(p)] for p in prange]
        v = [v_ref[0, sl, col(p)] for p in prange]
        kk = [kk_ref[0, sl, col(p)] for p in prange]
        a = [a_ref[0, sl, col(p)] for p in prange]
        lw = [lw_ref[0, sl, col(p)] for p in prange]
        lg = [_dot_exact_lhs(cum, lw[p]) for p in prange]
        lg_end = [lg[p][c - 1:c, :] for p in prange]
        e_inv = [jnp.exp(-lg[p]) for p in prange]
        e_end = [jnp.exp(lg_end[p] - lg[p]) for p in prange]
        ab = [kk[p] * a[p] for p in prange]
        kk_t = [kk[p] * jnp.exp(lg[p] - lw[p]) for p in prange]
        r_t = [r[p] * jnp.exp(lg[p]) for p in prange]
        b_t = [ab[p] * e_inv[p] for p in prange]
        k_t = [k[p] * e_inv[p] for p in prange]
        b_end = [ab[p] * e_end[p] for p in prange]
        k_end = [k[p] * e_end[p] for p in prange]
        kk_h = [jnp.where(hm, kk_t[p], 0.0) for p, hm in chains]
        r_h = [jnp.where(hm, r_t[p], 0.0) for p, hm in chains]
        v_h = [jnp.where(hm, v[p], 0.0) for p, hm in chains]
        nc = range(len(chains))
        pair_of = [p for p, _ in chains]
        l_b = [jnp.where(strict, _dot_nt(kk_h[j], b_t[pair_of[j]]), 0.0) for j in nc]
        l_k = [jnp.where(strict, _dot_nt(kk_h[j], k_t[pair_of[j]]), 0.0) for j in nc]
        a_rb = [jnp.where(incl, _dot_nt(r_h[j], b_t[pair_of[j]]), 0.0) for j in nc]
        a_rk = [jnp.where(incl, _dot_nt(r_h[j], k_t[pair_of[j]]), 0.0) for j in nc]
        pw = [-l_b[j] for j in nc]
        t_inv = [eye + pw[j] for j in nc]
        for _ in range(c.bit_length() - 2):
            pw = [_dot(pw[j], pw[j]) for j in nc]
            t_inv = [t_inv[j] + _dot(t_inv[j], pw[j]) for j in nc]
        lkv = [_dot(l_k[j], v_h[j]) for j in nc]
        w_h = [_dot(t_inv[j], kk_h[j]) for j in nc]
        yv_h = [_dot(a_rk[j], v_h[j]) for j in nc]
        uv_h = [_dot(t_inv[j], lkv[j]) for j in nc]
        both = lambda xs, p: xs[2 * p] + xs[2 * p + 1]
        zs = [_dot_nt(jnp.concatenate([both(w_h, p), r_t[p]], axis=0), state_ref[p]) for p in prange]
        u = [-(zs[p][:c] + both(uv_h, p)) for p in prange]
        yu = [_dot(a_rb[j], jnp.where(chains[j][1], u[pair_of[j]], 0.0)) for j in nc]
        upd = [_dot(jnp.concatenate([u[p], v[p]], axis=0).T, jnp.concatenate([b_end[p], k_end[p]], axis=0))
               for p in prange]
        for p in prange:
            state_ref[p] = state_ref[p] * jnp.exp(lg_end[p]) + jnp.where(same_head, upd[p], 0.0)
        for p in prange:
            y = zs[p][c:] + both(yv_h, p) + both(yu, p)
            mean = _head_sums(y) * (1.0 / HEAD_DIM)
            d = y - mean
            var = _head_sums(d * d) * (1.0 / HEAD_DIM)
            yn = d * lax.rsqrt(var + GN_EPS) * lnw_ref[:, col(p)] + lnb_ref[:, col(p)]
            bonus = _head_sums(r[p] * k[p] * rk_ref[:, col(p)]) * v[p]
            o_ref[0, sl, col(p)] = ((yn + bonus) * g_ref[0, sl, col(p)]).astype(o_ref.dtype)
        return 0

    lax.fori_loop(0, tile // c, chunk, 0)


def _rwscan(r, k, v, kk, a, lw, g, lnw, lnb, rk):
    b, s, _ = r.shape
    ts = ROW_TILE
    pairs = RWKV_WIDTH // LANES
    seq = pl.BlockSpec((1, ts, RWKV_WIDTH), lambda bb, i: (bb, i, 0))
    vec = pl.BlockSpec((1, RWKV_WIDTH), lambda bb, i: (0, 0))
    return pl.pallas_call(
        _rwscan_kernel,
        grid=(b, s // ts),
        in_specs=[seq] * 7 + [vec] * 3,
        out_specs=seq,
        out_shape=jax.ShapeDtypeStruct((b, s, RWKV_WIDTH), BF16),
        scratch_shapes=[pltpu.VMEM((pairs, LANES, LANES), F32)],
        compiler_params=_params(("parallel", "arbitrary")),
        name="rwscan",
    )(r, k, v, kk, a, lw, g, lnw, lnb, rk)


def _mixout_kernel(x_ref, ya_ref, yb_ref, zg_ref, woa_ref, wob_ref, wo_ref, fg_ref, rwt_ref, rb_ref,
                   x1_ref, h2_ref, idx_ref, gate_ref):
    ga = _sigmoid(zg_ref[:, :D_MODEL])
    gb = _sigmoid(zg_ref[:, D_MODEL:])
    merged = (ga * jnp.dot(ya_ref[...], woa_ref[...], preferred_element_type=F32)
              + gb * jnp.dot(yb_ref[...], wob_ref[...], preferred_element_type=F32))
    x1 = x_ref[...] + jnp.dot(merged.astype(BF16), wo_ref[...], preferred_element_type=F32)
    x1_ref[...] = x1
    h2 = _rms(x1, fg_ref[...])
    h2_ref[...] = h2.reshape(h2_ref.shape)
    logits = lax.dot_general(rwt_ref[...], h2, (((1,), (1,)), ((), ())),
                             precision=lax.Precision.HIGHEST, preferred_element_type=F32) + rb_ref[...]
    eid = lax.broadcasted_iota(jnp.int32, logits.shape, 0).astype(F32)
    vals, ids = [], []
    for _ in range(TOP_K):
        m = jnp.max(logits, axis=0, keepdims=True)
        sel = jnp.min(jnp.where(logits == m, eid, float(N_EXPERTS)), axis=0, keepdims=True)
        vals.append(m)
        ids.append(sel.astype(jnp.int32))
        logits = jnp.where(eid == sel, -jnp.inf, logits)
    ex = [jnp.exp(vv - vals[0]) for vv in vals]
    den = ex[0] + ex[1] + ex[2] + ex[3]
    idx_ref[...] = jnp.concatenate(ids, axis=0)
    gate_ref[...] = jnp.concatenate([e / den for e in ex], axis=0)


def _mixout(x2, ya, yb, zg, woa, wob, wo, fg, rwt, rb):
    n = x2.shape[0]
    tm = ROW_TILE
    full = lambda a: pl.BlockSpec(a.shape, lambda i: (0,) * a.ndim)
    rows = lambda c: pl.BlockSpec((tm, c), lambda i: (i, 0))
    cols = pl.BlockSpec((TOP_K, tm), lambda i: (0, i))
    return pl.pallas_call(
        _mixout_kernel,
        grid=(n // tm,),
        in_specs=[rows(D_MODEL), rows(SB_WIDTH), rows(RWKV_WIDTH), rows(2 * D_MODEL),
                  full(woa), full(wob), full(wo), full(fg), full(rwt), full(rb)],
        out_specs=[rows(D_MODEL), pl.BlockSpec((tm * TOKEN_TILE_ROWS, LANES), lambda i: (i, 0)), cols, cols],
        out_shape=[jax.ShapeDtypeStruct((n, D_MODEL), F32),
                   jax.ShapeDtypeStruct((n * TOKEN_TILE_ROWS, LANES), F32),
                   jax.ShapeDtypeStruct((TOP_K, n), jnp.int32),
                   jax.ShapeDtypeStruct((TOP_K, n), F32)],
        compiler_params=_params(("parallel",)),
        name="mixout",
    )(x2, ya, yb, zg, woa, wob, wo, fg, rwt, rb)


def _split_gate_up_kernel(w_ref, g_ref, u_ref):
    width = 2 * LANES
    rows = lax.broadcasted_iota(jnp.int32, (width, width), 0)
    cols = lax.broadcasted_iota(jnp.int32, (width, width), 1)
    src = jnp.where(cols < LANES, 2 * cols, 2 * (cols - LANES) + 1)
    sel = jnp.where(rows == src, 1.0, 0.0).astype(BF16)
    for c in range(w_ref.shape[2] // width):
        blk = w_ref[0, :, c * width:(c + 1) * width].astype(BF16)
        out = jnp.dot(blk, sel, preferred_element_type=F32)
        g_ref[0, :, c * LANES:(c + 1) * LANES] = out[:, :LANES].astype(BF16)
        u_ref[0, :, c * LANES:(c + 1) * LANES] = out[:, LANES:].astype(BF16)


def _split_gate_up(w1):
    e, d, f2 = w1.shape
    tr = 512
    out = pl.BlockSpec((1, tr, f2 // 2), lambda ee, i: (ee, i, 0))
    return pl.pallas_call(
        _split_gate_up_kernel,
        grid=(e, d // tr),
        in_specs=[pl.BlockSpec((1, tr, f2), lambda ee, i: (ee, i, 0))],
        out_specs=[out, out],
        out_shape=[jax.ShapeDtypeStruct((e, d, f2 // 2), BF16)] * 2,
        compiler_params=_params(("parallel", "parallel")),
        name="split_gate_up",
    )(w1)


def _moe_kernel(be_ref, nused_ref, tok0_ref, tokn_ref, dstp_ref, dstc_ref, wrow_ref, h_hbm,
                w1g_ref, w1u_ref, b1g_ref, b1u_ref, w2_ref, b2_ref, y_hbm,
                xbuf, ybuf, gsem, ssem):
    g = MOE_BLOCK
    i = pl.program_id(0)
    n_used = nused_ref[0]
    slot = lax.rem(i, 2)

    t = TOKEN_TILE_ROWS

    def gather(tok_ref, s):
        for r in range(g):
            src = h_hbm.at[pl.ds(pl.multiple_of(tok_ref[0, 0, r], t), t)]
            pltpu.make_async_copy(src, xbuf.at[s, pl.ds(r * t, t)], gsem.at[s]).start(priority=0)

    def scatter(dst_ref, s):
        for r in range(g):
            dst = y_hbm.at[pl.ds(pl.multiple_of(dst_ref[0, 0, r], t), t)]
            pltpu.make_async_copy(ybuf.at[s, pl.ds(r * t, t)], dst, ssem).start(priority=1)

    def slot_wait(buf, sem):
        pltpu.make_async_copy(buf, buf, sem).wait()

    @pl.when(i == 0)
    def _():
        ybuf[1] = jnp.zeros(ybuf.shape[1:], F32)
        gather(tok0_ref, 0)

    @pl.when(jnp.logical_and(i >= 1, i < n_used))
    def _():
        slot_wait(ybuf.at[slot], ssem)

    @pl.when(i < n_used)
    def _():
        slot_wait(xbuf.at[slot], gsem.at[slot])
        gather(tokn_ref, 1 - slot)
        scatter(dstp_ref, 1 - slot)
        xb = xbuf[slot].reshape(g, D_MODEL).astype(BF16)
        g_lin = jnp.dot(xb, w1g_ref[0], preferred_element_type=F32) + b1g_ref[0]
        u_lin = jnp.dot(xb, w1u_ref[0], preferred_element_type=F32) + b1u_ref[0]
        g_lin = jnp.minimum(g_lin, SWIGLU_LIMIT)
        u_lin = jnp.clip(u_lin, -SWIGLU_LIMIT, SWIGLU_LIMIT)
        glu = g_lin * _sigmoid(SWIGLU_ALPHA * g_lin)
        hmid = ((u_lin + 1.0) * glu).astype(BF16)
        y = jnp.dot(hmid, w2_ref[0], preferred_element_type=F32) + b2_ref[0]
        rr = lax.broadcasted_iota(jnp.int32, (g, g), 0)
        cc = lax.broadcasted_iota(jnp.int32, (g, g), 1)
        wdiag = jnp.where(rr == cc, jnp.broadcast_to(wrow_ref[0], (g, g)), 0.0)
        wcol = jnp.dot(wdiag, jnp.ones((g, LANES), F32), precision=lax.Precision.HIGHEST,
                       preferred_element_type=F32)
        ybuf[slot] = (y * jnp.tile(wcol, (1, D_MODEL // LANES))).reshape(ybuf.shape[1:])

    @pl.when(i == n_used - 1)
    def _():
        slot_wait(ybuf.at[1 - slot], ssem)
        scatter(dstc_ref, slot)
        slot_wait(ybuf.at[slot], ssem)
        slot_wait(xbuf.at[1 - slot], gsem.at[1 - slot])


def _moe(block_e, n_used, tok, dst, wrow, h2, w1g, w1u, b1g, b1u, w2, b2, n_assign):
    nb = tok.shape[0]
    g = MOE_BLOCK
    d = D_MODEL
    f = EXPERT_HIDDEN
    t = TOKEN_TILE_ROWS
    spare = n_assign + jnp.arange(g, dtype=jnp.int32).reshape(1, 1, g)
    dst = jnp.concatenate([spare, dst], axis=0) * t
    tok = tok * t
    smem = lambda im: pl.BlockSpec((1, 1, g), im, memory_space=pltpu.SMEM)
    ew = lambda shape: pl.BlockSpec((1,) + shape, lambda i, be, nu: (be[i], 0, 0))
    grid_spec = pltpu.PrefetchScalarGridSpec(
        num_scalar_prefetch=2,
        grid=(nb,),
        in_specs=[smem(lambda i, be, nu: (0, 0, 0)),
                  smem(lambda i, be, nu: (jnp.minimum(i + 1, nb - 1), 0, 0)),
                  smem(lambda i, be, nu: (i, 0, 0)),
                  smem(lambda i, be, nu: (i + 1, 0, 0)),
                  pl.BlockSpec((1, 1, g), lambda i, be, nu: (i, 0, 0)),
                  pl.BlockSpec(memory_space=pl.ANY),
                  ew((d, f)), ew((d, f)), ew((1, f)), ew((1, f)), ew((f, d)), ew((1, d))],
        out_specs=pl.BlockSpec(memory_space=pl.ANY),
        scratch_shapes=[pltpu.VMEM((2, g * t, LANES), F32), pltpu.VMEM((2, g * t, LANES), F32),
                        pltpu.SemaphoreType.DMA((2,)), pltpu.SemaphoreType.DMA(())],
    )
    return pl.pallas_call(
        _moe_kernel,
        grid_spec=grid_spec,
        out_shape=jax.ShapeDtypeStruct(((n_assign + g) * t, LANES), F32),
        compiler_params=_params(("arbitrary",)),
        name="moe",
    )(block_e, n_used, tok, tok, dst, dst, wrow, h2, w1g, w1u, b1g, b1u, w2, b2)


def _final_kernel(x1_ref, y0_ref, y1_ref, y2_ref, y3_ref, p_ref, pg_ref, wgate_ref, wproj_ref, fg_ref, o_ref,
                  *, last):
    y = y0_ref[...] + y1_ref[...] + y2_ref[...] + y3_ref[...]
    x2 = x1_ref[...] + y.reshape(x1_ref.shape)
    hp = _rms(x2, pg_ref[...]).astype(BF16)
    gate = _sigmoid(jnp.dot(hp, wgate_ref[...], preferred_element_type=F32))
    proj = jnp.dot(p_ref[...].astype(BF16), wproj_ref[...], preferred_element_type=F32)
    x3 = x2 + gate * proj
    o_ref[...] = _rms(x3, fg_ref[...]) if last else x3


def _final(x1, ycomb, p2, pg, wgate, wproj, fg, last):
    n = x1.shape[0]
    tm = ROW_TILE
    full = lambda a: pl.BlockSpec(a.shape, lambda i: (0,) * a.ndim)
    rows = lambda c: pl.BlockSpec((tm, c), lambda i: (i, 0))
    ysp = lambda k: pl.BlockSpec((tm * TOKEN_TILE_ROWS, LANES), lambda i: (k * (n // tm) + i, 0))
    return pl.pallas_call(
        functools.partial(_final_kernel, last=last),
        grid=(n // tm,),
        in_specs=[rows(D_MODEL), ysp(0), ysp(1), ysp(2), ysp(3), rows(PLE_DIM),
                  full(pg), full(wgate), full(wproj), full(fg)],
        out_specs=rows(D_MODEL),
        out_shape=jax.ShapeDtypeStruct((n, D_MODEL), F32),
        compiler_params=_params(("parallel",)),
        name="final",
    )(x1, ycomb, ycomb, ycomb, ycomb, p2, pg, wgate, wproj, fg)


def _dispatch(top_idx, gate):
    n = top_idx.shape[1]
    nk = n * TOP_K
    i32 = jnp.int32
    flat_e = top_idx.reshape(-1)
    order = jnp.argsort(flat_e).astype(i32)
    experts = jnp.arange(N_EXPERTS, dtype=i32)
    counts = jnp.sum((flat_e[:, None] == experts[None, :]).astype(i32), axis=0)
    offsets = jnp.cumsum(counts) - counts
    padded = (counts + MOE_BLOCK - 1) // MOE_BLOCK * MOE_BLOCK
    pad_end = jnp.cumsum(padded)
    pad_off = pad_end - padded
    n_blocks = -(-nk // MOE_BLOCK) + N_EXPERTS
    block_start = jnp.arange(n_blocks, dtype=i32) * MOE_BLOCK
    block_e = jnp.minimum(jnp.sum((pad_end[None, :] <= block_start[:, None]).astype(i32), axis=1), N_EXPERTS - 1)
    within = jnp.arange(MOE_BLOCK, dtype=i32)[None, :]
    rank = (block_start - pad_off[block_e])[:, None] + within
    valid = rank < counts[block_e][:, None]
    assign = order[jnp.clip(offsets[block_e][:, None] + rank, 0, nk - 1)]
    buf_tok = jnp.where(valid, assign % n, 0)
    buf_dst = jnp.where(valid, assign, nk + within)
    buf_w = jnp.where(valid, gate.reshape(-1)[assign], 0.0)
    n_used = (pad_end[-1] // MOE_BLOCK).astype(i32).reshape(1)
    shape = (n_blocks, 1, MOE_BLOCK)
    return block_e, n_used, buf_tok.reshape(shape), buf_dst.reshape(shape), buf_w.reshape(shape)


def _pad_rows(w, before, total):
    return jnp.zeros((total, w.shape[1]), w.dtype).at[before:before + w.shape[0]].set(w)


def kernel(x, p, mix_norm_g, w_in, rwkv_mu, rwkv_w0, rwkv_w_up, rwkv_a0, rwkv_a_up, rwkv_g_up, rwkv_k_k, rwkv_k_a, rwkv_r_k, rwkv_lnx_w, rwkv_lnx_b, w_out_a, w_out_b, w_out, ffn_norm_g, router_w, router_b, exp_w1, exp_b1, exp_w2, exp_b2, ple_norm_g, ple_gate_w, ple_proj_w, final_norm_g):
    b, s, d = x.shape
    n = b * s
    depth = w_in.shape[0]
    row = lambda t: t.reshape(1, -1)
    x2 = x.reshape(n, d)
    for i in range(depth):
        c_sb = 3 * SB_WIDTH
        c_rw = c_sb + RWKV_COLS
        w = w_in[i]
        wqkv = jnp.concatenate([w[:, :SB_WIDTH] * HEAD_DIM ** -0.5, w[:, SB_WIDTH:c_sb]], axis=1).astype(BF16)
        qkv, zr, zg = _inproj(x2, row(mix_norm_g[i]), wqkv, w[:, c_sb:c_rw].astype(BF16), w[:, c_rw:].astype(BF16))
        ya = _sbattn(qkv.reshape(b, s, c_sb))
        wup = _pad_rows(rwkv_w_up[i], 0, LANES)
        aup = _pad_rows(rwkv_a_up[i], DECAY_RANK, LANES)
        r, kh, v, kkn, a, lw, g = _rwprep(
            zr.reshape(b, s, RWKV_COLS), row(rwkv_mu[i]), row(rwkv_w0[i]), wup, row(rwkv_a0[i]), aup,
            rwkv_g_up[i], row(rwkv_k_k[i]), row(rwkv_k_a[i]))
        yb = _rwscan(r, kh, v, kkn, a, lw, g, row(rwkv_lnx_w[i]), row(rwkv_lnx_b[i]), row(rwkv_r_k[i]))
        x1, h2, top_idx, gate = _mixout(
            x2, ya.reshape(n, SB_WIDTH), yb.reshape(n, RWKV_WIDTH), zg,
            w_out_a[i].astype(BF16), w_out_b[i].astype(BF16), w_out[i].astype(BF16), row(ffn_norm_g[i]),
            router_w[i].T, router_b[i].reshape(N_EXPERTS, 1))
        block_e, n_used, buf_tok, buf_dst, buf_w = _dispatch(top_idx, gate)
        w1g, w1u = _split_gate_up(exp_w1[i])
        b1 = exp_b1[i]
        ycomb = _moe(block_e, n_used, buf_tok, buf_dst, buf_w, h2, w1g, w1u,
                     b1[:, None, 0::2], b1[:, None, 1::2], exp_w2[i].astype(BF16), exp_b2[i][:, None, :],
                     n * TOP_K)
        x2 = _final(x1, ycomb, p[i].reshape(n, PLE_DIM), row(ple_norm_g[i]), ple_gate_w[i].astype(BF16),
                    ple_proj_w[i].astype(BF16), row(final_norm_g), i == depth - 1)
    return x2.reshape(b, s, d)
```

```python
import functools

import jax
import jax.numpy as jnp
from jax import lax
from jax.experimental import pallas as pl
from jax.experimental.pallas import tpu as pltpu

F32 = jnp.float32
BF16 = jnp.bfloat16

D_MODEL = 1024
HEAD_DIM = 64
SB_WIDTH = 512
RWKV_WIDTH = 512
DECAY_RANK = 64
ICLR_RANK = 64
GATE_RANK = 128
RWKV_COLS = 3 * RWKV_WIDTH + DECAY_RANK + ICLR_RANK + GATE_RANK
PLE_DIM = 256
N_EXPERTS = 32
TOP_K = 4
EXPERT_HIDDEN = D_MODEL
SWIGLU_LIMIT = 7.0
SWIGLU_ALPHA = 1.702
MOE_BLOCK = 256
RMS_EPS = 1e-5
GN_EPS = 64e-5

LANES = 128
TOKEN_TILE_ROWS = D_MODEL // LANES
ROW_TILE = 256
ATTN_BLOCK = 256
ATTN_HALVES = 2
CHUNK = 64
RWSCAN_BATCH = 2
VMEM_LIMIT = 56 * 1024 * 1024


def _params(sem, vmem=VMEM_LIMIT):
    return pltpu.CompilerParams(dimension_semantics=sem, vmem_limit_bytes=vmem)


def _dot(a, b):
    return jnp.dot(a.astype(BF16), b.astype(BF16), preferred_element_type=F32)


def _dot_nt(a, b):
    return lax.dot_general(a.astype(BF16), b.astype(BF16), (((1,), (1,)), ((), ())),
                           preferred_element_type=F32)


def _split(a):
    hi = a.astype(BF16)
    lo = (a - hi.astype(F32)).astype(BF16)
    return hi, lo


def _dot_exact_lhs(mask_bf16, a):
    hi, lo = _split(a)
    return (jnp.dot(mask_bf16, hi, preferred_element_type=F32)
            + jnp.dot(mask_bf16, lo, preferred_element_type=F32))


def _softplus(z):
    return jnp.maximum(z, 0.0) + jnp.log(1.0 + jnp.exp(-jnp.abs(z)))


def _sigmoid(z):
    return 1.0 / (1.0 + jnp.exp(-z))


def _rms(x, g):
    ms = jnp.mean(x * x, axis=-1, keepdims=True)
    return x * lax.rsqrt(ms + RMS_EPS) * g


def _inproj_kernel(x_ref, g_ref, wqkv_ref, wr_ref, wg_ref, qkv_ref, zr_ref, zg_ref):
    h = _rms(x_ref[...], g_ref[...]).astype(BF16)
    qkv_ref[...] = jnp.dot(h, wqkv_ref[...], preferred_element_type=F32).astype(BF16)
    zr_ref[...] = jnp.dot(h, wr_ref[...], preferred_element_type=F32)
    zg_ref[...] = jnp.dot(h, wg_ref[...], preferred_element_type=F32)


def _inproj(x2, g, wqkv, wr, wg):
    n = x2.shape[0]
    tm = ROW_TILE
    full = lambda a: pl.BlockSpec(a.shape, lambda i: (0,) * a.ndim)
    rows = lambda c: pl.BlockSpec((tm, c), lambda i: (i, 0))
    return pl.pallas_call(
        _inproj_kernel,
        grid=(n // tm,),
        in_specs=[rows(D_MODEL), full(g), full(wqkv), full(wr), full(wg)],
        out_specs=[rows(3 * SB_WIDTH), rows(RWKV_COLS), rows(2 * D_MODEL)],
        out_shape=[jax.ShapeDtypeStruct((n, 3 * SB_WIDTH), BF16),
                   jax.ShapeDtypeStruct((n, RWKV_COLS), F32),
                   jax.ShapeDtypeStruct((n, 2 * D_MODEL), F32)],
        compiler_params=_params(("parallel",)),
        name="inproj",
    )(x2, g, wqkv, wr, wg)


def _sbattn_kernel(q_ref, k_ref, v_ref, o_ref):
    tb = ATTN_BLOCK
    i = pl.program_id(2)
    lane = lax.broadcasted_iota(jnp.int32, (tb, LANES), 1)
    row = lax.broadcasted_iota(jnp.int32, (tb, tb), 0)
    col = lax.broadcasted_iota(jnp.int32, (tb, tb), 1)
    causal = col < row
    later = jnp.where(row > col, 1.0, 0.0).astype(BF16)

    def kv(j):
        at = pl.ds(pl.multiple_of(j * tb, tb), tb)
        return k_ref[0, at, :], v_ref[0, at, :]

    def blocks(qhs, kvs, states, diag):
        n = range(len(qhs))
        z = [lax.dot_general(qhs[c], kvs[c][0], (((1,), (1,)), ((), ())), preferred_element_type=F32) for c in n]
        sp = [_softplus(z[c]) for c in n]
        l1 = [jnp.where(causal, -sp[c], 0.0) if diag else -sp[c] for c in n]
        after = [jnp.dot(l1[c].astype(BF16), later, preferred_element_type=F32) for c in n]
        att = [jnp.exp((z[c] - sp[c]) + after[c] + states[c][0]) for c in n]
        if diag:
            att = [jnp.where(causal, att[c], 0.0) for c in n]
        acc = [states[c][1] + jnp.dot(att[c].astype(BF16), kvs[c][1], preferred_element_type=F32) for c in n]
        carry = [states[c][0] + jnp.sum(l1[c], axis=1, keepdims=True) for c in n]
        return [(carry[c], acc[c]) for c in n]

    zero = (jnp.zeros((tb, 1), F32), jnp.zeros((tb, LANES), F32))
    qs, kv_diag = [], []
    for half in range(ATTN_HALVES):
        q = q_ref[0, half * tb:(half + 1) * tb, :]
        for head in range(2):
            in_head = (lane < HEAD_DIM) if head == 0 else (lane >= HEAD_DIM)
            qs.append(jnp.where(in_head, q, jnp.zeros_like(q)))
            kv_diag.append(kv(ATTN_HALVES * i + half))
    states = blocks(qs, kv_diag, [zero] * 4, True)
    states[2:] = blocks(qs[2:], [kv(ATTN_HALVES * i)] * 2, states[2:], False)

    def body(t, st):
        return tuple(blocks(qs, [kv(ATTN_HALVES * i - 1 - t)] * 4, list(st), False))

    states = lax.fori_loop(0, ATTN_HALVES * i, body, tuple(states))
    for half in range(ATTN_HALVES):
        o_ref[0, half * tb:(half + 1) * tb, :] = jnp.where(
            lane < HEAD_DIM, states[2 * half][1], states[2 * half + 1][1]).astype(BF16)


def _sbattn(qkv3):
    b, s, _ = qkv3.shape
    tb = ATTN_BLOCK * ATTN_HALVES
    pairs = SB_WIDTH // LANES
    return pl.pallas_call(
        _sbattn_kernel,
        grid=(b, pairs, s // tb),
        in_specs=[pl.BlockSpec((1, tb, LANES), lambda bb, p, i: (bb, i, p)),
                  pl.BlockSpec((1, s, LANES), lambda bb, p, i: (bb, 0, pairs + p)),
                  pl.BlockSpec((1, s, LANES), lambda bb, p, i: (bb, 0, 2 * pairs + p))],
        out_specs=pl.BlockSpec((1, tb, LANES), lambda bb, p, i: (bb, i, p)),
        out_shape=jax.ShapeDtypeStruct((b, s, SB_WIDTH), BF16),
        compiler_params=_params(("parallel", "parallel", "arbitrary")),
        name="sbattn",
    )(qkv3, qkv3, qkv3)


def _head_sums(t):
    lane = lax.broadcasted_iota(jnp.int32, t.shape, 1)
    low = lane < HEAD_DIM
    s_low = jnp.sum(jnp.where(low, t, 0.0), axis=1, keepdims=True)
    s_high = jnp.sum(jnp.where(low, 0.0, t), axis=1, keepdims=True)
    return jnp.where(low, s_low, s_high)


def _rwprep_kernel(z_ref, zp_ref, mu_ref, w0_ref, wup_ref, a0_ref, aup_ref, gup_ref, kk_ref, ka_ref,
                   r_ref, k_ref, v_ref, kkn_ref, a_ref, lw_ref, g_ref):
    ts = z_ref.shape[1]
    i = pl.program_id(1)
    z = z_ref[0]
    rowid = lax.broadcasted_iota(jnp.int32, z.shape, 0)
    last = jnp.where(i == 0, 0.0, zp_ref[0, 7:8, :])
    prev = jnp.where(rowid == 0, last, pltpu.roll(z, 1, 0))
    zs = z + mu_ref[...] * (prev - z)
    c1, c2, c3 = RWKV_WIDTH, 2 * RWKV_WIDTH, 3 * RWKV_WIDTH
    r, k, v = zs[:, :c1], zs[:, c1:c2], zs[:, c2:c3]
    xwa = zs[:, c3:c3 + LANES]
    xg = zs[:, c3 + LANES:]
    hp = lax.Precision.HIGHEST
    w_pre = w0_ref[...] + jnp.dot(jnp.tanh(xwa), wup_ref[...], precision=hp, preferred_element_type=F32)
    w_log = -_softplus(-w_pre) - 0.5
    lw = -jnp.exp(w_log)
    a = _sigmoid(a0_ref[...] + jnp.dot(xwa, aup_ref[...], precision=hp, preferred_element_type=F32))
    g = jnp.dot(_sigmoid(xg), gup_ref[...], precision=hp, preferred_element_type=F32)
    kk = k * kk_ref[...]
    for t in range(RWKV_WIDTH // LANES):
        sl = slice(t * LANES, (t + 1) * LANES)
        kt = kk[:, sl]
        nrm = jnp.sqrt(_head_sums(kt * kt))
        kkn_ref[0, :, sl] = kt / jnp.maximum(nrm, 1e-12)
    r_ref[0] = r
    k_ref[0] = k * (1.0 + (a - 1.0) * ka_ref[...])
    v_ref[0] = v
    a_ref[0] = a
    lw_ref[0] = lw
    g_ref[0] = g


def _rwprep(zr3, mu, w0, wup, a0, aup, gup, k_k, k_a):
    b, s, _ = zr3.shape
    ts = ROW_TILE
    full = lambda a: pl.BlockSpec(a.shape, lambda bb, i: (0,) * a.ndim)
    out = pl.BlockSpec((1, ts, RWKV_WIDTH), lambda bb, i: (bb, i, 0))
    return pl.pallas_call(
        _rwprep_kernel,
        grid=(b, s // ts),
        in_specs=[pl.BlockSpec((1, ts, RWKV_COLS), lambda bb, i: (bb, i, 0)),
                  pl.BlockSpec((1, 8, RWKV_COLS), lambda bb, i: (bb, jnp.maximum(i * (ts // 8) - 1, 0), 0)),
                  full(mu), full(w0), full(wup), full(a0), full(aup), full(gup), full(k_k), full(k_a)],
        out_specs=[out] * 7,
        out_shape=[jax.ShapeDtypeStruct((b, s, RWKV_WIDTH), F32)] * 7,
        compiler_params=_params(("parallel", "parallel")),
        name="rwprep",
    )(zr3, zr3, mu, w0, wup, a0, aup, gup, k_k, k_a)


def _rwscan_kernel(r_ref, k_ref, v_ref, kk_ref, a_ref, lw_ref, g_ref, lnw_ref, lnb_ref, rk_ref, o_ref, state_ref):
    c = CHUNK
    tile = r_ref.shape[1]
    pairs = RWKV_WIDTH // LANES

    @pl.when(pl.program_id(1) == 0)
    def _():
        state_ref[...] = jnp.zeros_like(state_ref)

    lane = lax.broadcasted_iota(jnp.int32, (c, LANES), 1)
    heads = (lane < HEAD_DIM, lane >= HEAD_DIM)
    trow = lax.broadcasted_iota(jnp.int32, (c, c), 0)
    tcol = lax.broadcasted_iota(jnp.int32, (c, c), 1)
    strict = tcol < trow
    incl = tcol <= trow
    cum = jnp.where(incl, 1.0, 0.0).astype(BF16)
    srow = lax.broadcasted_iota(jnp.int32, (LANES, LANES), 0) < HEAD_DIM
    scol = lax.broadcasted_iota(jnp.int32, (LANES, LANES), 1) < HEAD_DIM
    same_head = srow == scol
    eye = jnp.where(trow == tcol, 1.0, 0.0)
    prange = range(r_ref.shape[0] * pairs)
    chains = [(p, hm) for p in prange for hm in heads]

    def chunk(ci, _):
        sl = pl.ds(pl.multiple_of(ci * c, c), c)
        col = lambda p: slice((p % pairs) * LANES, (p % pairs + 1) * LANES)
        seq = lambda ref, p: ref[p // pairs, sl, col(p)]
        r = [seq(r_ref, p) for p in prange]
        k = [seq(k_ref, p) for p in prange]
        v = [seq(v_ref, p) for p in prange]
        kk = [seq(kk_ref, p) for p in prange]
        a = [seq(a_ref, p) for p in prange]
        lw = [seq(lw_ref, p) for p in prange]
        lg = [_dot_exact_lhs(cum, lw[p]) for p in prange]
        lg_end = [lg[p][c - 1:c, :] for p in prange]
        e_inv = [jnp.exp(-lg[p]) for p in prange]
        e_end = [jnp.exp(lg_end[p] - lg[p]) for p in prange]
        ab = [kk[p] * a[p] for p in prange]
        kk_t = [kk[p] * jnp.exp(lg[p] - lw[p]) for p in prange]
        r_t = [r[p] * jnp.exp(lg[p]) for p in prange]
        b_t = [ab[p] * e_inv[p] for p in prange]
        k_t = [k[p] * e_inv[p] for p in prange]
        b_end = [ab[p] * e_end[p] for p in prange]
        k_end = [k[p] * e_end[p] for p in prange]
        kk_h = [jnp.where(hm, kk_t[p], 0.0) for p, hm in chains]
        r_h = [jnp.where(hm, r_t[p], 0.0) for p, hm in chains]
        v_h = [jnp.where(hm, v[p], 0.0) for p, hm in chains]
        nc = range(len(chains))
        pair_of = [p for p, _ in chains]
        l_b = [jnp.where(strict, _dot_nt(kk_h[j], b_t[pair_of[j]]), 0.0) for j in nc]
        l_k = [jnp.where(strict, _dot_nt(kk_h[j], k_t[pair_of[j]]), 0.0) for j in nc]
        a_rb = [jnp.where(incl, _dot_nt(r_h[j], b_t[pair_of[j]]), 0.0) for j in nc]
        a_rk = [jnp.where(incl, _dot_nt(r_h[j], k_t[pair_of[j]]), 0.0) for j in nc]
        pw = [-l_b[j] for j in nc]
        t_inv = [eye + pw[j] for j in nc]
        for _ in range(c.bit_length() - 2):
            pw = [_dot(pw[j], pw[j]) for j in nc]
            t_inv = [t_inv[j] + _dot(t_inv[j], pw[j]) for j in nc]
        lkv = [_dot(l_k[j], v_h[j]) for j in nc]
        w_h = [_dot(t_inv[j], kk_h[j]) for j in nc]
        yv_h = [_dot(a_rk[j], v_h[j]) for j in nc]
        uv_h = [_dot(t_inv[j], lkv[j]) for j in nc]
        both = lambda xs, p: xs[2 * p] + xs[2 * p + 1]
        zs = [_dot_nt(jnp.concatenate([both(w_h, p), r_t[p]], axis=0), state_ref[p]) for p in prange]
        u = [-(zs[p][:c] + both(uv_h, p)) for p in prange]
        yu = [_dot(a_rb[j], jnp.where(chains[j][1], u[pair_of[j]], 0.0)) for j in nc]
        upd = [_dot(jnp.concatenate([u[p], v[p]], axis=0).T, jnp.concatenate([b_end[p], k_end[p]], axis=0))
               for p in prange]
        for p in prange:
            state_ref[p] = state_ref[p] * jnp.exp(lg_end[p]) + jnp.where(same_head, upd[p], 0.0)
        for p in prange:
            y = zs[p][c:] + both(yv_h, p) + both(yu, p)
            mean = _head_sums(y) * (1.0 / HEAD_DIM)
            d = y - mean
            var = _head_sums(d * d) * (1.0 / HEAD_DIM)
            yn = d * lax.rsqrt(var + GN_EPS) * lnw_ref[:, col(p)] + lnb_ref[:, col(p)]
            bonus = _head_sums(r[p] * k[p] * rk_ref[:, col(p)]) * v[p]
            o_ref[p // pairs, sl, col(p)] = ((yn + bonus) * seq(g_ref, p)).astype(o_ref.dtype)
        return 0

    lax.fori_loop(0, tile // c, chunk, 0)


def _rwscan(r, k, v, kk, a, lw, g, lnw, lnb, rk):
    b, s, _ = r.shape
    ts = ROW_TILE
    pairs = RWKV_WIDTH // LANES
    nb = RWSCAN_BATCH
    seq = pl.BlockSpec((nb, ts, RWKV_WIDTH), lambda bb, i: (bb, i, 0))
    vec = pl.BlockSpec((1, RWKV_WIDTH), lambda bb, i: (0, 0))
    return pl.pallas_call(
        _rwscan_kernel,
        grid=(b // nb, s // ts),
        in_specs=[seq] * 7 + [vec] * 3,
        out_specs=seq,
        out_shape=jax.ShapeDtypeStruct((b, s, RWKV_WIDTH), BF16),
        scratch_shapes=[pltpu.VMEM((nb * pairs, LANES, LANES), F32)],
        compiler_params=_params(("parallel", "arbitrary")),
        name="rwscan",
    )(r, k, v, kk, a, lw, g, lnw, lnb, rk)


def _mixout_kernel(x_ref, ya_ref, yb_ref, zg_ref, woa_ref, wob_ref, wo_ref, fg_ref, rwt_ref, rb_ref,
                   x1_ref, h2_ref, idx_ref, gate_ref):
    ga = _sigmoid(zg_ref[:, :D_MODEL])
    gb = _sigmoid(zg_ref[:, D_MODEL:])
    merged = (ga * jnp.dot(ya_ref[...], woa_ref[...], preferred_element_type=F32)
              + gb * jnp.dot(yb_ref[...], wob_ref[...], preferred_element_type=F32))
    x1 = x_ref[...] + jnp.dot(merged.astype(BF16), wo_ref[...], preferred_element_type=F32)
    x1_ref[...] = x1
    h2 = _rms(x1, fg_ref[...])
    h2_ref[...] = h2.reshape(h2_ref.shape)
    logits = lax.dot_general(rwt_ref[...], h2, (((1,), (1,)), ((), ())),
                             precision=lax.Precision.HIGHEST, preferred_element_type=F32) + rb_ref[...]
    eid = lax.broadcasted_iota(jnp.int32, logits.shape, 0).astype(F32)
    vals, ids = [], []
    for _ in range(TOP_K):
        m = jnp.max(logits, axis=0, keepdims=True)
        sel = jnp.min(jnp.where(logits == m, eid, float(N_EXPERTS)), axis=0, keepdims=True)
        vals.append(m)
        ids.append(sel.astype(jnp.int32))
        logits = jnp.where(eid == sel, -jnp.inf, logits)
    ex = [jnp.exp(vv - vals[0]) for vv in vals]
    den = ex[0] + ex[1] + ex[2] + ex[3]
    idx_ref[...] = jnp.concatenate(ids, axis=0)
    gate_ref[...] = jnp.concatenate([e / den for e in ex], axis=0)


def _mixout(x2, ya, yb, zg, woa, wob, wo, fg, rwt, rb):
    n = x2.shape[0]
    tm = ROW_TILE
    full = lambda a: pl.BlockSpec(a.shape, lambda i: (0,) * a.ndim)
    rows = lambda c: pl.BlockSpec((tm, c), lambda i: (i, 0))
    cols = pl.BlockSpec((TOP_K, tm), lambda i: (0, i))
    return pl.pallas_call(
        _mixout_kernel,
        grid=(n // tm,),
        in_specs=[rows(D_MODEL), rows(SB_WIDTH), rows(RWKV_WIDTH), rows(2 * D_MODEL),
                  full(woa), full(wob), full(wo), full(fg), full(rwt), full(rb)],
        out_specs=[rows(D_MODEL), pl.BlockSpec((tm * TOKEN_TILE_ROWS, LANES), lambda i: (i, 0)), cols, cols],
        out_shape=[jax.ShapeDtypeStruct((n, D_MODEL), F32),
                   jax.ShapeDtypeStruct((n * TOKEN_TILE_ROWS, LANES), F32),
                   jax.ShapeDtypeStruct((TOP_K, n), jnp.int32),
                   jax.ShapeDtypeStruct((TOP_K, n), F32)],
        compiler_params=_params(("parallel",)),
        name="mixout",
    )(x2, ya, yb, zg, woa, wob, wo, fg, rwt, rb)


def _split_gate_up_kernel(w_ref, g_ref, u_ref):
    width = 2 * LANES
    rows = lax.broadcasted_iota(jnp.int32, (width, width), 0)
    cols = lax.broadcasted_iota(jnp.int32, (width, width), 1)
    src = jnp.where(cols < LANES, 2 * cols, 2 * (cols - LANES) + 1)
    sel = jnp.where(rows == src, 1.0, 0.0).astype(BF16)
    for c in range(w_ref.shape[2] // width):
        blk = w_ref[0, :, c * width:(c + 1) * width].astype(BF16)
        out = jnp.dot(blk, sel, preferred_element_type=F32)
        g_ref[0, :, c * LANES:(c + 1) * LANES] = out[:, :LANES].astype(BF16)
        u_ref[0, :, c * LANES:(c + 1) * LANES] = out[:, LANES:].astype(BF16)


def _split_gate_up(w1):
    e, d, f2 = w1.shape
    tr = 512
    out = pl.BlockSpec((1, tr, f2 // 2), lambda ee, i: (ee, i, 0))
    return pl.pallas_call(
        _split_gate_up_kernel,
        grid=(e, d // tr),
        in_specs=[pl.BlockSpec((1, tr, f2), lambda ee, i: (ee, i, 0))],
        out_specs=[out, out],
        out_shape=[jax.ShapeDtypeStruct((e, d, f2 // 2), BF16)] * 2,
        compiler_params=_params(("parallel", "parallel")),
        name="split_gate_up",
    )(w1)


def _moe_kernel(be_ref, nused_ref, tok0_ref, tokn_ref, dstp_ref, dstc_ref, wrow_ref, h_hbm,
                w1g_ref, w1u_ref, b1g_ref, b1u_ref, w2_ref, b2_ref, y_hbm,
                xbuf, ybuf, gsem, ssem):
    g = MOE_BLOCK
    i = pl.program_id(0)
    n_used = nused_ref[0]
    t = TOKEN_TILE_ROWS

    def gather(tok_ref, s):
        for r in range(g):
            src = h_hbm.at[pl.ds(pl.multiple_of(tok_ref[0, 0, r], t), t)]
            pltpu.make_async_copy(src, xbuf.at[s, pl.ds(r * t, t)], gsem.at[s]).start(priority=0)

    def scatter(dst_ref, s):
        for r in range(g):
            dst = y_hbm.at[pl.ds(pl.multiple_of(dst_ref[0, 0, r], t), t)]
            pltpu.make_async_copy(ybuf.at[s, pl.ds(r * t, t)], dst, ssem).start(priority=1)

    def slot_wait(buf, sem):
        pltpu.make_async_copy(buf, buf, sem).wait()

    @pl.when(i == 0)
    def _():
        ybuf[1] = jnp.zeros(ybuf.shape[1:], F32)
        gather(tok0_ref, 0)

    slot = lax.rem(i, 2)

    @pl.when(jnp.logical_and(i >= 1, i < n_used))
    def _():
        slot_wait(ybuf.at[slot], ssem)

    @pl.when(i < n_used)
    def _():
        slot_wait(xbuf.at[slot], gsem.at[slot])
        gather(tokn_ref, 1 - slot)
        scatter(dstp_ref, 1 - slot)
        xb = xbuf[slot].reshape(g, D_MODEL).astype(BF16)
        g_lin = jnp.dot(xb, w1g_ref[0], preferred_element_type=F32) + b1g_ref[0]
        u_lin = jnp.dot(xb, w1u_ref[0], preferred_element_type=F32) + b1u_ref[0]
        g_lin = jnp.minimum(g_lin, SWIGLU_LIMIT)
        u_lin = jnp.clip(u_lin, -SWIGLU_LIMIT, SWIGLU_LIMIT)
        glu = g_lin * _sigmoid(SWIGLU_ALPHA * g_lin)
        hmid = ((u_lin + 1.0) * glu).astype(BF16)
        y = jnp.dot(hmid, w2_ref[0], preferred_element_type=F32) + b2_ref[0]
        rr = lax.broadcasted_iota(jnp.int32, (g, g), 0)
        cc = lax.broadcasted_iota(jnp.int32, (g, g), 1)
        wdiag = jnp.where(rr == cc, jnp.broadcast_to(wrow_ref[0], (g, g)), 0.0)
        wcol = jnp.dot(wdiag, jnp.ones((g, LANES), F32), precision=lax.Precision.HIGHEST,
                       preferred_element_type=F32)
        ybuf[slot] = (y * jnp.tile(wcol, (1, D_MODEL // LANES))).reshape(ybuf.shape[1:])

    @pl.when(i == n_used - 1)
    def _():
        slot_wait(ybuf.at[1 - slot], ssem)
        scatter(dstc_ref, slot)
        slot_wait(ybuf.at[slot], ssem)
        slot_wait(xbuf.at[1 - slot], gsem.at[1 - slot])


def _moe(block_e, n_used, tok, dst, wrow, h2, w1g, w1u, b1g, b1u, w2, b2, n_assign):
    nb = tok.shape[0]
    g = MOE_BLOCK
    d = D_MODEL
    f = EXPERT_HIDDEN
    t = TOKEN_TILE_ROWS
    spare = n_assign + jnp.arange(g, dtype=jnp.int32).reshape(1, 1, g)
    dst = jnp.concatenate([spare, dst], axis=0) * t
    tok = tok * t
    smem = lambda im: pl.BlockSpec((1, 1, g), im, memory_space=pltpu.SMEM)
    ew = lambda shape: pl.BlockSpec((1,) + shape, lambda i, be, nu: (be[i], 0, 0))
    grid_spec = pltpu.PrefetchScalarGridSpec(
        num_scalar_prefetch=2,
        grid=(nb,),
        in_specs=[smem(lambda i, be, nu: (0, 0, 0)),
                  smem(lambda i, be, nu: (jnp.minimum(i + 1, nb - 1), 0, 0)),
                  smem(lambda i, be, nu: (i, 0, 0)),
                  smem(lambda i, be, nu: (i + 1, 0, 0)),
                  pl.BlockSpec((1, 1, g), lambda i, be, nu: (i, 0, 0)),
                  pl.BlockSpec(memory_space=pl.ANY),
                  ew((d, f)), ew((d, f)), ew((1, f)), ew((1, f)), ew((f, d)), ew((1, d))],
        out_specs=pl.BlockSpec(memory_space=pl.ANY),
        scratch_shapes=[pltpu.VMEM((2, g * t, LANES), F32), pltpu.VMEM((2, g * t, LANES), F32),
                        pltpu.SemaphoreType.DMA((2,)), pltpu.SemaphoreType.DMA(())],
    )
    return pl.pallas_call(
        _moe_kernel,
        grid_spec=grid_spec,
        out_shape=jax.ShapeDtypeStruct(((n_assign + g) * t, LANES), F32),
        compiler_params=_params(("arbitrary",)),
        name="moe",
    )(block_e, n_used, tok, tok, dst, dst, wrow, h2, w1g, w1u, b1g, b1u, w2, b2)


def _final_kernel(x1_ref, y0_ref, y1_ref, y2_ref, y3_ref, p_ref, pg_ref, wgate_ref, wproj_ref, fg_ref, o_ref,
                  *, last):
    y = y0_ref[...] + y1_ref[...] + y2_ref[...] + y3_ref[...]
    x2 = x1_ref[...] + y.reshape(x1_ref.shape)
    hp = _rms(x2, pg_ref[...]).astype(BF16)
    gate = _sigmoid(jnp.dot(hp, wgate_ref[...], preferred_element_type=F32))
    proj = jnp.dot(p_ref[...].astype(BF16), wproj_ref[...], preferred_element_type=F32)
    x3 = x2 + gate * proj
    o_ref[...] = _rms(x3, fg_ref[...]) if last else x3


def _final(x1, ycomb, p2, pg, wgate, wproj, fg, last):
    n = x1.shape[0]
    tm = ROW_TILE
    full = lambda a: pl.BlockSpec(a.shape, lambda i: (0,) * a.ndim)
    rows = lambda c: pl.BlockSpec((tm, c), lambda i: (i, 0))
    ysp = lambda k: pl.BlockSpec((tm * TOKEN_TILE_ROWS, LANES), lambda i: (k * (n // tm) + i, 0))
    return pl.pallas_call(
        functools.partial(_final_kernel, last=last),
        grid=(n // tm,),
        in_specs=[rows(D_MODEL), ysp(0), ysp(1), ysp(2), ysp(3), rows(PLE_DIM),
                  full(pg), full(wgate), full(wproj), full(fg)],
        out_specs=rows(D_MODEL),
        out_shape=jax.ShapeDtypeStruct((n, D_MODEL), F32),
        compiler_params=_params(("parallel",)),
        name="final",
    )(x1, ycomb, ycomb, ycomb, ycomb, p2, pg, wgate, wproj, fg)


def _dispatch(top_idx, gate):
    n = top_idx.shape[1]
    nk = n * TOP_K
    i32 = jnp.int32
    flat_e = top_idx.reshape(-1)
    order = jnp.argsort(flat_e).astype(i32)
    experts = jnp.arange(N_EXPERTS, dtype=i32)
    counts = jnp.sum((flat_e[:, None] == experts[None, :]).astype(i32), axis=0)
    offsets = jnp.cumsum(counts) - counts
    padded = (counts + MOE_BLOCK - 1) // MOE_BLOCK * MOE_BLOCK
    pad_end = jnp.cumsum(padded)
    pad_off = pad_end - padded
    n_blocks = -(-nk // MOE_BLOCK) + N_EXPERTS
    block_start = jnp.arange(n_blocks, dtype=i32) * MOE_BLOCK
    block_e = jnp.minimum(jnp.sum((pad_end[None, :] <= block_start[:, None]).astype(i32), axis=1), N_EXPERTS - 1)
    within = jnp.arange(MOE_BLOCK, dtype=i32)[None, :]
    rank = (block_start - pad_off[block_e])[:, None] + within
    valid = rank < counts[block_e][:, None]
    assign = order[jnp.clip(offsets[block_e][:, None] + rank, 0, nk - 1)]
    buf_tok = jnp.where(valid, assign % n, 0)
    buf_dst = jnp.where(valid, assign, nk + within)
    buf_w = jnp.where(valid, gate.reshape(-1)[assign], 0.0)
    n_used = (pad_end[-1] // MOE_BLOCK).astype(i32).reshape(1)
    shape = (n_blocks, 1, MOE_BLOCK)
    return block_e, n_used, buf_tok.reshape(shape), buf_dst.reshape(shape), buf_w.reshape(shape)


def _pad_rows(w, before, total):
    return jnp.zeros((total, w.shape[1]), w.dtype).at[before:before + w.shape[0]].set(w)


def kernel(x, p, mix_norm_g, w_in, rwkv_mu, rwkv_w0, rwkv_w_up, rwkv_a0, rwkv_a_up, rwkv_g_up, rwkv_k_k, rwkv_k_a, rwkv_r_k, rwkv_lnx_w, rwkv_lnx_b, w_out_a, w_out_b, w_out, ffn_norm_g, router_w, router_b, exp_w1, exp_b1, exp_w2, exp_b2, ple_norm_g, ple_gate_w, ple_proj_w, final_norm_g):
    b, s, d = x.shape
    n = b * s
    depth = w_in.shape[0]
    row = lambda t: t.reshape(1, -1)
    x2 = x.reshape(n, d)
    for i in range(depth):
        c_sb = 3 * SB_WIDTH
        c_rw = c_sb + RWKV_COLS
        w = w_in[i]
        wqkv = jnp.concatenate([w[:, :SB_WIDTH] * HEAD_DIM ** -0.5, w[:, SB_WIDTH:c_sb]], axis=1).astype(BF16)
        qkv, zr, zg = _inproj(x2, row(mix_norm_g[i]), wqkv, w[:, c_sb:c_rw].astype(BF16), w[:, c_rw:].astype(BF16))
        ya = _sbattn(qkv.reshape(b, s, c_sb))
        wup = _pad_rows(rwkv_w_up[i], 0, LANES)
        aup = _pad_rows(rwkv_a_up[i], DECAY_RANK, LANES)
        r, kh, v, kkn, a, lw, g = _rwprep(
            zr.reshape(b, s, RWKV_COLS), row(rwkv_mu[i]), row(rwkv_w0[i]), wup, row(rwkv_a0[i]), aup,
            rwkv_g_up[i], row(rwkv_k_k[i]), row(rwkv_k_a[i]))
        yb = _rwscan(r, kh, v, kkn, a, lw, g, row(rwkv_lnx_w[i]), row(rwkv_lnx_b[i]), row(rwkv_r_k[i]))
        x1, h2, top_idx, gate = _mixout(
            x2, ya.reshape(n, SB_WIDTH), yb.reshape(n, RWKV_WIDTH), zg,
            w_out_a[i].astype(BF16), w_out_b[i].astype(BF16), w_out[i].astype(BF16), row(ffn_norm_g[i]),
            router_w[i].T, router_b[i].reshape(N_EXPERTS, 1))
        block_e, n_used, buf_tok, buf_dst, buf_w = _dispatch(top_idx, gate)
        w1g, w1u = _split_gate_up(exp_w1[i])
        b1 = exp_b1[i]
        ycomb = _moe(block_e, n_used, buf_tok, buf_dst, buf_w, h2, w1g, w1u,
                     b1[:, None, 0::2], b1[:, None, 1::2], exp_w2[i].astype(BF16), exp_b2[i][:, None, :],
                     n * TOP_K)
        x2 = _final(x1, ycomb, p[i].reshape(n, PLE_DIM), row(ple_norm_g[i]), ple_gate_w[i].astype(BF16),
                    ple_proj_w[i].astype(BF16), row(final_norm_g), i == depth - 1)
    return x2.reshape(b, s, d)
```

```python
import functools

import jax
import jax.numpy as jnp
from jax import lax
from jax.experimental import pallas as pl
from jax.experimental.pallas import tpu as pltpu

F32 = jnp.float32
BF16 = jnp.bfloat16

D_MODEL = 1024
HEAD_DIM = 64
SB_WIDTH = 512
RWKV_WIDTH = 512
DECAY_RANK = 64
ICLR_RANK = 64
GATE_RANK = 128
RWKV_COLS = 3 * RWKV_WIDTH + DECAY_RANK + ICLR_RANK + GATE_RANK
PLE_DIM = 256
N_EXPERTS = 32
TOP_K = 4
EXPERT_HIDDEN = D_MODEL
SWIGLU_LIMIT = 7.0
SWIGLU_ALPHA = 1.702
MOE_BLOCK = 256
MOE_HIDDEN_CHUNK = 256
RMS_EPS = 1e-5
GN_EPS = 64e-5

LANES = 128
TOKEN_TILE_ROWS = D_MODEL // LANES
ROW_TILE = 512
ATTN_BLOCK = 256
ATTN_HALVES = 2
CHUNK = 64
RWSCAN_BATCH = 2
VMEM_LIMIT = 56 * 1024 * 1024


def _params(sem, vmem=VMEM_LIMIT):
    return pltpu.CompilerParams(dimension_semantics=sem, vmem_limit_bytes=vmem)


def _dot(a, b):
    return jnp.dot(a.astype(BF16), b.astype(BF16), preferred_element_type=F32)


def _dot_nt(a, b):
    return lax.dot_general(a.astype(BF16), b.astype(BF16), (((1,), (1,)), ((), ())),
                           preferred_element_type=F32)


def _split(a):
    hi = a.astype(BF16)
    lo = (a - hi.astype(F32)).astype(BF16)
    return hi, lo


def _dot_exact_lhs(mask_bf16, a):
    hi, lo = _split(a)
    return (jnp.dot(mask_bf16, hi, preferred_element_type=F32)
            + jnp.dot(mask_bf16, lo, preferred_element_type=F32))


def _softplus(z):
    return jnp.maximum(z, 0.0) + jnp.log(1.0 + jnp.exp(-jnp.abs(z)))


def _sigmoid(z):
    return 1.0 / (1.0 + jnp.exp(-z))


def _rms(x, g):
    ms = jnp.mean(x * x, axis=-1, keepdims=True)
    return x * lax.rsqrt(ms + RMS_EPS) * g


def _inproj_kernel(x_ref, g_ref, wqkv_ref, wr_ref, wg_ref, qkv_ref, zr_ref, zg_ref):
    h = _rms(x_ref[...], g_ref[...]).astype(BF16)
    qkv_ref[...] = jnp.dot(h, wqkv_ref[...], preferred_element_type=F32).astype(BF16)
    zr_ref[...] = jnp.dot(h, wr_ref[...], preferred_element_type=F32)
    zg_ref[...] = jnp.dot(h, wg_ref[...], preferred_element_type=F32)


def _inproj(x2, g, wqkv, wr, wg):
    n = x2.shape[0]
    tm = ROW_TILE
    full = lambda a: pl.BlockSpec(a.shape, lambda i: (0,) * a.ndim)
    rows = lambda c: pl.BlockSpec((tm, c), lambda i: (i, 0))
    return pl.pallas_call(
        _inproj_kernel,
        grid=(n // tm,),
        in_specs=[rows(D_MODEL), full(g), full(wqkv), full(wr), full(wg)],
        out_specs=[rows(3 * SB_WIDTH), rows(RWKV_COLS), rows(2 * D_MODEL)],
        out_shape=[jax.ShapeDtypeStruct((n, 3 * SB_WIDTH), BF16),
                   jax.ShapeDtypeStruct((n, RWKV_COLS), F32),
                   jax.ShapeDtypeStruct((n, 2 * D_MODEL), F32)],
        compiler_params=_params(("parallel",)),
        name="inproj",
    )(x2, g, wqkv, wr, wg)


def _sbattn_kernel(q_ref, k_ref, v_ref, o_ref):
    tb = ATTN_BLOCK
    i = pl.program_id(2)
    lane = lax.broadcasted_iota(jnp.int32, (tb, LANES), 1)
    row = lax.broadcasted_iota(jnp.int32, (tb, tb), 0)
    col = lax.broadcasted_iota(jnp.int32, (tb, tb), 1)
    causal = col < row
    later = jnp.where(row > col, 1.0, 0.0).astype(BF16)

    def kv(j):
        at = pl.ds(pl.multiple_of(j * tb, tb), tb)
        return k_ref[0, at, :], v_ref[0, at, :]

    def blocks(qhs, kvs, states, diag):
        n = range(len(qhs))
        z = [lax.dot_general(qhs[c], kvs[c][0], (((1,), (1,)), ((), ())), preferred_element_type=F32) for c in n]
        sp = [_softplus(z[c]) for c in n]
        l1 = [jnp.where(causal, -sp[c], 0.0) if diag else -sp[c] for c in n]
        after = [jnp.dot(l1[c].astype(BF16), later, preferred_element_type=F32) for c in n]
        att = [jnp.exp((z[c] - sp[c]) + after[c] + states[c][0]) for c in n]
        if diag:
            att = [jnp.where(causal, att[c], 0.0) for c in n]
        acc = [states[c][1] + jnp.dot(att[c].astype(BF16), kvs[c][1], preferred_element_type=F32) for c in n]
        carry = [states[c][0] + jnp.sum(l1[c], axis=1, keepdims=True) for c in n]
        return [(carry[c], acc[c]) for c in n]

    zero = (jnp.zeros((tb, 1), F32), jnp.zeros((tb, LANES), F32))
    qs, kv_diag = [], []
    for half in range(ATTN_HALVES):
        q = q_ref[0, half * tb:(half + 1) * tb, :]
        for head in range(2):
            in_head = (lane < HEAD_DIM) if head == 0 else (lane >= HEAD_DIM)
            qs.append(jnp.where(in_head, q, jnp.zeros_like(q)))
            kv_diag.append(kv(ATTN_HALVES * i + half))
    states = blocks(qs, kv_diag, [zero] * 4, True)
    states[2:] = blocks(qs[2:], [kv(ATTN_HALVES * i)] * 2, states[2:], False)

    def body(t, st):
        return tuple(blocks(qs, [kv(ATTN_HALVES * i - 1 - t)] * 4, list(st), False))

    states = lax.fori_loop(0, ATTN_HALVES * i, body, tuple(states))
    for half in range(ATTN_HALVES):
        o_ref[0, half * tb:(half + 1) * tb, :] = jnp.where(
            lane < HEAD_DIM, states[2 * half][1], states[2 * half + 1][1]).astype(BF16)


def _sbattn(qkv3):
    b, s, _ = qkv3.shape
    tb = ATTN_BLOCK * ATTN_HALVES
    pairs = SB_WIDTH // LANES
    return pl.pallas_call(
        _sbattn_kernel,
        grid=(b, pairs, s // tb),
        in_specs=[pl.BlockSpec((1, tb, LANES), lambda bb, p, i: (bb, i, p)),
                  pl.BlockSpec((1, s, LANES), lambda bb, p, i: (bb, 0, pairs + p)),
                  pl.BlockSpec((1, s, LANES), lambda bb, p, i: (bb, 0, 2 * pairs + p))],
        out_specs=pl.BlockSpec((1, tb, LANES), lambda bb, p, i: (bb, i, p)),
        out_shape=jax.ShapeDtypeStruct((b, s, SB_WIDTH), BF16),
        compiler_params=_params(("parallel", "parallel", "arbitrary")),
        name="sbattn",
    )(qkv3, qkv3, qkv3)


def _head_sums(t):
    lane = lax.broadcasted_iota(jnp.int32, t.shape, 1)
    low = lane < HEAD_DIM
    s_low = jnp.sum(jnp.where(low, t, 0.0), axis=1, keepdims=True)
    s_high = jnp.sum(jnp.where(low, 0.0, t), axis=1, keepdims=True)
    return jnp.where(low, s_low, s_high)


def _rwprep_kernel(z_ref, zp_ref, mu_ref, w0_ref, wup_ref, a0_ref, aup_ref, gup_ref, kk_ref, ka_ref,
                   r_ref, k_ref, v_ref, kkn_ref, a_ref, lw_ref, g_ref):
    ts = z_ref.shape[1]
    i = pl.program_id(1)
    z = z_ref[0]
    rowid = lax.broadcasted_iota(jnp.int32, z.shape, 0)
    last = jnp.where(i == 0, 0.0, zp_ref[0, 7:8, :])
    prev = jnp.where(rowid == 0, last, pltpu.roll(z, 1, 0))
    zs = z + mu_ref[...] * (prev - z)
    c1, c2, c3 = RWKV_WIDTH, 2 * RWKV_WIDTH, 3 * RWKV_WIDTH
    r, k, v = zs[:, :c1], zs[:, c1:c2], zs[:, c2:c3]
    xwa = zs[:, c3:c3 + LANES]
    xg = zs[:, c3 + LANES:]
    hp = lax.Precision.HIGHEST
    w_pre = w0_ref[...] + jnp.dot(jnp.tanh(xwa), wup_ref[...], precision=hp, preferred_element_type=F32)
    w_log = -_softplus(-w_pre) - 0.5
    lw = -jnp.exp(w_log)
    a = _sigmoid(a0_ref[...] + jnp.dot(xwa, aup_ref[...], precision=hp, preferred_element_type=F32))
    g = jnp.dot(_sigmoid(xg), gup_ref[...], precision=hp, preferred_element_type=F32)
    kk = k * kk_ref[...]
    for t in range(RWKV_WIDTH // LANES):
        sl = slice(t * LANES, (t + 1) * LANES)
        kt = kk[:, sl]
        nrm = jnp.sqrt(_head_sums(kt * kt))
        kkn_ref[0, :, sl] = kt / jnp.maximum(nrm, 1e-12)
    r_ref[0] = r
    k_ref[0] = k * (1.0 + (a - 1.0) * ka_ref[...])
    v_ref[0] = v
    a_ref[0] = a
    lw_ref[0] = lw
    g_ref[0] = g


def _rwprep(zr3, mu, w0, wup, a0, aup, gup, k_k, k_a):
    b, s, _ = zr3.shape
    ts = ROW_TILE
    full = lambda a: pl.BlockSpec(a.shape, lambda bb, i: (0,) * a.ndim)
    out = pl.BlockSpec((1, ts, RWKV_WIDTH), lambda bb, i: (bb, i, 0))
    return pl.pallas_call(
        _rwprep_kernel,
        grid=(b, s // ts),
        in_specs=[pl.BlockSpec((1, ts, RWKV_COLS), lambda bb, i: (bb, i, 0)),
                  pl.BlockSpec((1, 8, RWKV_COLS), lambda bb, i: (bb, jnp.maximum(i * (ts // 8) - 1, 0), 0)),
                  full(mu), full(w0), full(wup), full(a0), full(aup), full(gup), full(k_k), full(k_a)],
        out_specs=[out] * 7,
        out_shape=[jax.ShapeDtypeStruct((b, s, RWKV_WIDTH), F32)] * 7,
        compiler_params=_params(("parallel", "parallel")),
        name="rwprep",
    )(zr3, zr3, mu, w0, wup, a0, aup, gup, k_k, k_a)


def _rwscan_kernel(r_ref, k_ref, v_ref, kk_ref, a_ref, lw_ref, g_ref, lnw_ref, lnb_ref, rk_ref, o_ref, state_ref):
    c = CHUNK
    tile = r_ref.shape[1]
    pairs = RWKV_WIDTH // LANES

    @pl.when(pl.program_id(1) == 0)
    def _():
        state_ref[...] = jnp.zeros_like(state_ref)

    lane = lax.broadcasted_iota(jnp.int32, (c, LANES), 1)
    heads = (lane < HEAD_DIM, lane >= HEAD_DIM)
    trow = lax.broadcasted_iota(jnp.int32, (c, c), 0)
    tcol = lax.broadcasted_iota(jnp.int32, (c, c), 1)
    strict = tcol < trow
    incl = tcol <= trow
    cum = jnp.where(incl, 1.0, 0.0).astype(BF16)
    srow = lax.broadcasted_iota(jnp.int32, (LANES, LANES), 0) < HEAD_DIM
    scol = lax.broadcasted_iota(jnp.int32, (LANES, LANES), 1) < HEAD_DIM
    same_head = srow == scol
    eye = jnp.where(trow == tcol, 1.0, 0.0)
    prange = range(r_ref.shape[0] * pairs)
    chains = [(p, hm) for p in prange for hm in heads]

    def chunk(ci, _):
        sl = pl.ds(pl.multiple_of(ci * c, c), c)
        col = lambda p: slice((p % pairs) * LANES, (p % pairs + 1) * LANES)
        seq = lambda ref, p: ref[p // pairs, sl, col(p)]
        r = [seq(r_ref, p) for p in prange]
        k = [seq(k_ref, p) for p in prange]
        v = [seq(v_ref, p) for p in prange]
        kk = [seq(kk_ref, p) for p in prange]
        a = [seq(a_ref, p) for p in prange]
        lw = [seq(lw_ref, p) for p in prange]
        lg = [_dot_exact_lhs(cum, lw[p]) for p in prange]
        lg_end = [lg[p][c - 1:c, :] for p in prange]
        e_inv = [jnp.exp(-lg[p]) for p in prange]
        e_end = [jnp.exp(lg_end[p] - lg[p]) for p in prange]
        ab = [kk[p] * a[p] for p in prange]
        kk_t = [kk[p] * jnp.exp(lg[p] - lw[p]) for p in prange]
        r_t = [r[p] * jnp.exp(lg[p]) for p in prange]
        b_t = [ab[p] * e_inv[p] for p in prange]
        k_t = [k[p] * e_inv[p] for p in prange]
        b_end = [ab[p] * e_end[p] for p in prange]
        k_end = [k[p] * e_end[p] for p in prange]
        kk_h = [jnp.where(hm, kk_t[p], 0.0) for p, hm in chains]
        r_h = [jnp.where(hm, r_t[p], 0.0) for p, hm in chains]
        v_h = [jnp.where(hm, v[p], 0.0) for p, hm in chains]
        nc = range(len(chains))
        pair_of = [p for p, _ in chains]
        l_b = [jnp.where(strict, _dot_nt(kk_h[j], b_t[pair_of[j]]), 0.0) for j in nc]
        l_k = [jnp.where(strict, _dot_nt(kk_h[j], k_t[pair_of[j]]), 0.0) for j in nc]
        a_rb = [jnp.where(incl, _dot_nt(r_h[j], b_t[pair_of[j]]), 0.0) for j in nc]
        a_rk = [jnp.where(incl, _dot_nt(r_h[j], k_t[pair_of[j]]), 0.0) for j in nc]
        pw = [-l_b[j] for j in nc]
        t_inv = [eye + pw[j] for j in nc]
        for _ in range(c.bit_length() - 2):
            pw = [_dot(pw[j], pw[j]) for j in nc]
            t_inv = [t_inv[j] + _dot(t_inv[j], pw[j]) for j in nc]
        lkv = [_dot(l_k[j], v_h[j]) for j in nc]
        w_h = [_dot(t_inv[j], kk_h[j]) for j in nc]
        yv_h = [_dot(a_rk[j], v_h[j]) for j in nc]
        uv_h = [_dot(t_inv[j], lkv[j]) for j in nc]
        both = lambda xs, p: xs[2 * p] + xs[2 * p + 1]
        zs = [_dot_nt(jnp.concatenate([both(w_h, p), r_t[p]], axis=0), state_ref[p]) for p in prange]
        u = [-(zs[p][:c] + both(uv_h, p)) for p in prange]
        yu = [_dot(a_rb[j], jnp.where(chains[j][1], u[pair_of[j]], 0.0)) for j in nc]
        upd = [_dot(jnp.concatenate([u[p], v[p]], axis=0).T, jnp.concatenate([b_end[p], k_end[p]], axis=0))
               for p in prange]
        for p in prange:
            state_ref[p] = state_ref[p] * jnp.exp(lg_end[p]) + jnp.where(same_head, upd[p], 0.0)
        for p in prange:
            y = zs[p][c:] + both(yv_h, p) + both(yu, p)
            mean = _head_sums(y) * (1.0 / HEAD_DIM)
            d = y - mean
            var = _head_sums(d * d) * (1.0 / HEAD_DIM)
            yn = d * lax.rsqrt(var + GN_EPS) * lnw_ref[:, col(p)] + lnb_ref[:, col(p)]
            bonus = _head_sums(r[p] * k[p] * rk_ref[:, col(p)]) * v[p]
            o_ref[p // pairs, sl, col(p)] = ((yn + bonus) * seq(g_ref, p)).astype(o_ref.dtype)
        return 0

    lax.fori_loop(0, tile // c, chunk, 0)


def _rwscan(r, k, v, kk, a, lw, g, lnw, lnb, rk):
    b, s, _ = r.shape
    ts = ROW_TILE
    pairs = RWKV_WIDTH // LANES
    nb = RWSCAN_BATCH
    seq = pl.BlockSpec((nb, ts, RWKV_WIDTH), lambda bb, i: (bb, i, 0))
    vec = pl.BlockSpec((1, RWKV_WIDTH), lambda bb, i: (0, 0))
    return pl.pallas_call(
        _rwscan_kernel,
        grid=(b // nb, s // ts),
        in_specs=[seq] * 7 + [vec] * 3,
        out_specs=seq,
        out_shape=jax.ShapeDtypeStruct((b, s, RWKV_WIDTH), BF16),
        scratch_shapes=[pltpu.VMEM((nb * pairs, LANES, LANES), F32)],
        compiler_params=_params(("parallel", "arbitrary")),
        name="rwscan",
    )(r, k, v, kk, a, lw, g, lnw, lnb, rk)


def _mixout_kernel(x_ref, ya_ref, yb_ref, zg_ref, woa_ref, wob_ref, wo_ref, fg_ref, rwt_ref, rb_ref,
                   x1_ref, h2_ref, idx_ref, gate_ref):
    ga = _sigmoid(zg_ref[:, :D_MODEL])
    gb = _sigmoid(zg_ref[:, D_MODEL:])
    merged = (ga * jnp.dot(ya_ref[...], woa_ref[...], preferred_element_type=F32)
              + gb * jnp.dot(yb_ref[...], wob_ref[...], preferred_element_type=F32))
    x1 = x_ref[...] + jnp.dot(merged.astype(BF16), wo_ref[...], preferred_element_type=F32)
    x1_ref[...] = x1
    h2 = _rms(x1, fg_ref[...])
    h2_ref[...] = h2.reshape(h2_ref.shape)
    logits = lax.dot_general(rwt_ref[...], h2, (((1,), (1,)), ((), ())),
                             precision=lax.Precision.HIGHEST, preferred_element_type=F32) + rb_ref[...]
    eid = lax.broadcasted_iota(jnp.int32, logits.shape, 0).astype(F32)
    vals, ids = [], []
    for _ in range(TOP_K):
        m = jnp.max(logits, axis=0, keepdims=True)
        sel = jnp.min(jnp.where(logits == m, eid, float(N_EXPERTS)), axis=0, keepdims=True)
        vals.append(m)
        ids.append(sel.astype(jnp.int32))
        logits = jnp.where(eid == sel, -jnp.inf, logits)
    ex = [jnp.exp(vv - vals[0]) for vv in vals]
    den = ex[0] + ex[1] + ex[2] + ex[3]
    idx_ref[...] = jnp.concatenate(ids, axis=0)
    gate_ref[...] = jnp.concatenate([e / den for e in ex], axis=0)


def _mixout(x2, ya, yb, zg, woa, wob, wo, fg, rwt, rb):
    n = x2.shape[0]
    tm = ROW_TILE
    full = lambda a: pl.BlockSpec(a.shape, lambda i: (0,) * a.ndim)
    rows = lambda c: pl.BlockSpec((tm, c), lambda i: (i, 0))
    cols = pl.BlockSpec((TOP_K, tm), lambda i: (0, i))
    return pl.pallas_call(
        _mixout_kernel,
        grid=(n // tm,),
        in_specs=[rows(D_MODEL), rows(SB_WIDTH), rows(RWKV_WIDTH), rows(2 * D_MODEL),
                  full(woa), full(wob), full(wo), full(fg), full(rwt), full(rb)],
        out_specs=[rows(D_MODEL), pl.BlockSpec((tm * TOKEN_TILE_ROWS, LANES), lambda i: (i, 0)), cols, cols],
        out_shape=[jax.ShapeDtypeStruct((n, D_MODEL), F32),
                   jax.ShapeDtypeStruct((n * TOKEN_TILE_ROWS, LANES), F32),
                   jax.ShapeDtypeStruct((TOP_K, n), jnp.int32),
                   jax.ShapeDtypeStruct((TOP_K, n), F32)],
        compiler_params=_params(("parallel",)),
        name="mixout",
    )(x2, ya, yb, zg, woa, wob, wo, fg, rwt, rb)


def _split_gate_up_kernel(w_ref, g_ref, u_ref):
    width = 2 * LANES
    rows = lax.broadcasted_iota(jnp.int32, (width, width), 0)
    cols = lax.broadcasted_iota(jnp.int32, (width, width), 1)
    src = jnp.where(cols < LANES, 2 * cols, 2 * (cols - LANES) + 1)
    sel = jnp.where(rows == src, 1.0, 0.0).astype(BF16)
    for c in range(w_ref.shape[2] // width):
        blk = w_ref[0, :, c * width:(c + 1) * width].astype(BF16)
        out = jnp.dot(blk, sel, preferred_element_type=F32)
        g_ref[0, :, c * LANES:(c + 1) * LANES] = out[:, :LANES].astype(BF16)
        u_ref[0, :, c * LANES:(c + 1) * LANES] = out[:, LANES:].astype(BF16)


def _split_gate_up(w1):
    e, d, f2 = w1.shape
    tr = 512
    out = pl.BlockSpec((1, tr, f2 // 2), lambda ee, i: (ee, i, 0))
    return pl.pallas_call(
        _split_gate_up_kernel,
        grid=(e, d // tr),
        in_specs=[pl.BlockSpec((1, tr, f2), lambda ee, i: (ee, i, 0))],
        out_specs=[out, out],
        out_shape=[jax.ShapeDtypeStruct((e, d, f2 // 2), BF16)] * 2,
        compiler_params=_params(("parallel", "parallel")),
        name="split_gate_up",
    )(w1)


def _moe_kernel(be_ref, nused_ref, tok0_ref, tokn_ref, dstp_ref, dstc_ref, wrow_ref, h_hbm,
                w1g_ref, w1u_ref, b1g_ref, b1u_ref, w2_ref, b2_ref, y_hbm,
                xbuf, ybuf, gsem, ssem):
    g = MOE_BLOCK
    i = pl.program_id(0)
    n_used = nused_ref[0]
    t = TOKEN_TILE_ROWS

    def gather(tok_ref, s):
        for r in range(g):
            src = h_hbm.at[pl.ds(pl.multiple_of(tok_ref[0, 0, r], t), t)]
            pltpu.make_async_copy(src, xbuf.at[s, pl.ds(r * t, t)], gsem.at[s]).start(priority=0)

    def scatter(dst_ref, s):
        for r in range(g):
            dst = y_hbm.at[pl.ds(pl.multiple_of(dst_ref[0, 0, r], t), t)]
            pltpu.make_async_copy(ybuf.at[s, pl.ds(r * t, t)], dst, ssem).start(priority=1)

    def slot_wait(buf, sem):
        pltpu.make_async_copy(buf, buf, sem).wait()

    @pl.when(i == 0)
    def _():
        ybuf[1] = jnp.zeros(ybuf.shape[1:], F32)
        gather(tok0_ref, 0)

    slot = lax.rem(i, 2)

    @pl.when(jnp.logical_and(i >= 1, i < n_used))
    def _():
        slot_wait(ybuf.at[slot], ssem)

    @pl.when(i < n_used)
    def _():
        slot_wait(xbuf.at[slot], gsem.at[slot])
        gather(tokn_ref, 1 - slot)
        scatter(dstp_ref, 1 - slot)
        xb = xbuf[slot].reshape(g, D_MODEL).astype(BF16)
        fc = MOE_HIDDEN_CHUNK
        n_chunks = EXPERT_HIDDEN // fc

        def up(j):
            cs = slice(j * fc, (j + 1) * fc)
            return (jnp.dot(xb, w1g_ref[0, :, cs], preferred_element_type=F32) + b1g_ref[0, :, cs],
                    jnp.dot(xb, w1u_ref[0, :, cs], preferred_element_type=F32) + b1u_ref[0, :, cs])

        def act(g_lin, u_lin):
            g_lin = jnp.minimum(g_lin, SWIGLU_LIMIT)
            u_lin = jnp.clip(u_lin, -SWIGLU_LIMIT, SWIGLU_LIMIT)
            glu = g_lin * _sigmoid(SWIGLU_ALPHA * g_lin)
            return ((u_lin + 1.0) * glu).astype(BF16)

        y = jnp.broadcast_to(b2_ref[0], (g, D_MODEL))
        pending = up(0)
        for j in range(n_chunks):
            ahead = up(j + 1) if j + 1 < n_chunks else None
            hmid = act(*pending)
            y = y + jnp.dot(hmid, w2_ref[0, j * fc:(j + 1) * fc, :], preferred_element_type=F32)
            pending = ahead
        rr = lax.broadcasted_iota(jnp.int32, (g, g), 0)
        cc = lax.broadcasted_iota(jnp.int32, (g, g), 1)
        wdiag = jnp.where(rr == cc, jnp.broadcast_to(wrow_ref[0], (g, g)), 0.0)
        wcol = jnp.dot(wdiag, jnp.ones((g, LANES), F32), precision=lax.Precision.HIGHEST,
                       preferred_element_type=F32)
        ybuf[slot] = (y * jnp.tile(wcol, (1, D_MODEL // LANES))).reshape(ybuf.shape[1:])

    @pl.when(i == n_used - 1)
    def _():
        slot_wait(ybuf.at[1 - slot], ssem)
        scatter(dstc_ref, slot)
        slot_wait(ybuf.at[slot], ssem)
        slot_wait(xbuf.at[1 - slot], gsem.at[1 - slot])


def _moe(block_e, n_used, tok, dst, wrow, h2, w1g, w1u, b1g, b1u, w2, b2, n_assign):
    nb = tok.shape[0]
    g = MOE_BLOCK
    d = D_MODEL
    f = EXPERT_HIDDEN
    t = TOKEN_TILE_ROWS
    spare = n_assign + jnp.arange(g, dtype=jnp.int32).reshape(1, 1, g)
    dst = jnp.concatenate([spare, dst], axis=0) * t
    tok = tok * t
    smem = lambda im: pl.BlockSpec((1, 1, g), im, memory_space=pltpu.SMEM)
    ew = lambda shape: pl.BlockSpec((1,) + shape, lambda i, be, nu: (be[i], 0, 0))
    grid_spec = pltpu.PrefetchScalarGridSpec(
        num_scalar_prefetch=2,
        grid=(nb,),
        in_specs=[smem(lambda i, be, nu: (0, 0, 0)),
                  smem(lambda i, be, nu: (jnp.minimum(i + 1, nb - 1), 0, 0)),
                  smem(lambda i, be, nu: (i, 0, 0)),
                  smem(lambda i, be, nu: (i + 1, 0, 0)),
                  pl.BlockSpec((1, 1, g), lambda i, be, nu: (i, 0, 0)),
                  pl.BlockSpec(memory_space=pl.ANY),
                  ew((d, f)), ew((d, f)), ew((1, f)), ew((1, f)), ew((f, d)), ew((1, d))],
        out_specs=pl.BlockSpec(memory_space=pl.ANY),
        scratch_shapes=[pltpu.VMEM((2, g * t, LANES), F32), pltpu.VMEM((2, g * t, LANES), F32),
                        pltpu.SemaphoreType.DMA((2,)), pltpu.SemaphoreType.DMA(())],
    )
    return pl.pallas_call(
        _moe_kernel,
        grid_spec=grid_spec,
        out_shape=jax.ShapeDtypeStruct(((n_assign + g) * t, LANES), F32),
        compiler_params=_params(("arbitrary",)),
        name="moe",
    )(block_e, n_used, tok, tok, dst, dst, wrow, h2, w1g, w1u, b1g, b1u, w2, b2)


def _final_kernel(x1_ref, y0_ref, y1_ref, y2_ref, y3_ref, p_ref, pg_ref, wgate_ref, wproj_ref, fg_ref, o_ref,
                  *, last):
    y = y0_ref[...] + y1_ref[...] + y2_ref[...] + y3_ref[...]
    x2 = x1_ref[...] + y.reshape(x1_ref.shape)
    hp = _rms(x2, pg_ref[...]).astype(BF16)
    gate = _sigmoid(jnp.dot(hp, wgate_ref[...], preferred_element_type=F32))
    proj = jnp.dot(p_ref[...].astype(BF16), wproj_ref[...], preferred_element_type=F32)
    x3 = x2 + gate * proj
    o_ref[...] = _rms(x3, fg_ref[...]) if last else x3


def _final(x1, ycomb, p2, pg, wgate, wproj, fg, last):
    n = x1.shape[0]
    tm = ROW_TILE
    full = lambda a: pl.BlockSpec(a.shape, lambda i: (0,) * a.ndim)
    rows = lambda c: pl.BlockSpec((tm, c), lambda i: (i, 0))
    ysp = lambda k: pl.BlockSpec((tm * TOKEN_TILE_ROWS, LANES), lambda i: (k * (n // tm) + i, 0))
    return pl.pallas_call(
        functools.partial(_final_kernel, last=last),
        grid=(n // tm,),
        in_specs=[rows(D_MODEL), ysp(0), ysp(1), ysp(2), ysp(3), rows(PLE_DIM),
                  full(pg), full(wgate), full(wproj), full(fg)],
        out_specs=rows(D_MODEL),
        out_shape=jax.ShapeDtypeStruct((n, D_MODEL), F32),
        compiler_params=_params(("parallel",)),
        name="final",
    )(x1, ycomb, ycomb, ycomb, ycomb, p2, pg, wgate, wproj, fg)


def _dispatch(top_idx, gate):
    n = top_idx.shape[1]
    nk = n * TOP_K
    i32 = jnp.int32
    flat_e = top_idx.reshape(-1)
    order = jnp.argsort(flat_e).astype(i32)
    experts = jnp.arange(N_EXPERTS, dtype=i32)
    counts = jnp.sum((flat_e[:, None] == experts[None, :]).astype(i32), axis=0)
    offsets = jnp.cumsum(counts) - counts
    padded = (counts + MOE_BLOCK - 1) // MOE_BLOCK * MOE_BLOCK
    pad_end = jnp.cumsum(padded)
    pad_off = pad_end - padded
    n_blocks = -(-nk // MOE_BLOCK) + N_EXPERTS
    block_start = jnp.arange(n_blocks, dtype=i32) * MOE_BLOCK
    block_e = jnp.minimum(jnp.sum((pad_end[None, :] <= block_start[:, None]).astype(i32), axis=1), N_EXPERTS - 1)
    within = jnp.arange(MOE_BLOCK, dtype=i32)[None, :]
    rank = (block_start - pad_off[block_e])[:, None] + within
    valid = rank < counts[block_e][:, None]
    assign = order[jnp.clip(offsets[block_e][:, None] + rank, 0, nk - 1)]
    buf_tok = jnp.where(valid, assign % n, 0)
    buf_dst = jnp.where(valid, assign, nk + within)
    buf_w = jnp.where(valid, gate.reshape(-1)[assign], 0.0)
    n_used = (pad_end[-1] // MOE_BLOCK).astype(i32).reshape(1)
    shape = (n_blocks, 1, MOE_BLOCK)
    return block_e, n_used, buf_tok.reshape(shape), buf_dst.reshape(shape), buf_w.reshape(shape)


def _pad_rows(w, before, total):
    return jnp.zeros((total, w.shape[1]), w.dtype).at[before:before + w.shape[0]].set(w)


def kernel(x, p, mix_norm_g, w_in, rwkv_mu, rwkv_w0, rwkv_w_up, rwkv_a0, rwkv_a_up, rwkv_g_up, rwkv_k_k, rwkv_k_a, rwkv_r_k, rwkv_lnx_w, rwkv_lnx_b, w_out_a, w_out_b, w_out, ffn_norm_g, router_w, router_b, exp_w1, exp_b1, exp_w2, exp_b2, ple_norm_g, ple_gate_w, ple_proj_w, final_norm_g):
    b, s, d = x.shape
    n = b * s
    depth = w_in.shape[0]
    row = lambda t: t.reshape(1, -1)
    x2 = x.reshape(n, d)
    for i in range(depth):
        c_sb = 3 * SB_WIDTH
        c_rw = c_sb + RWKV_COLS
        w = w_in[i]
        wqkv = jnp.concatenate([w[:, :SB_WIDTH] * HEAD_DIM ** -0.5, w[:, SB_WIDTH:c_sb]], axis=1).astype(BF16)
        qkv, zr, zg = _inproj(x2, row(mix_norm_g[i]), wqkv, w[:, c_sb:c_rw].astype(BF16), w[:, c_rw:].astype(BF16))
        ya = _sbattn(qkv.reshape(b, s, c_sb))
        wup = _pad_rows(rwkv_w_up[i], 0, LANES)
        aup = _pad_rows(rwkv_a_up[i], DECAY_RANK, LANES)
        r, kh, v, kkn, a, lw, g = _rwprep(
            zr.reshape(b, s, RWKV_COLS), row(rwkv_mu[i]), row(rwkv_w0[i]), wup, row(rwkv_a0[i]), aup,
            rwkv_g_up[i], row(rwkv_k_k[i]), row(rwkv_k_a[i]))
        yb = _rwscan(r, kh, v, kkn, a, lw, g, row(rwkv_lnx_w[i]), row(rwkv_lnx_b[i]), row(rwkv_r_k[i]))
        x1, h2, top_idx, gate = _mixout(
            x2, ya.reshape(n, SB_WIDTH), yb.reshape(n, RWKV_WIDTH), zg,
            w_out_a[i].astype(BF16), w_out_b[i].astype(BF16), w_out[i].astype(BF16), row(ffn_norm_g[i]),
            router_w[i].T, router_b[i].reshape(N_EXPERTS, 1))
        block_e, n_used, buf_tok, buf_dst, buf_w = _dispatch(top_idx, gate)
        w1g, w1u = _split_gate_up(exp_w1[i])
        b1 = exp_b1[i]
        ycomb = _moe(block_e, n_used, buf_tok, buf_dst, buf_w, h2, w1g, w1u,
                     b1[:, None, 0::2], b1[:, None, 1::2], exp_w2[i].astype(BF16), exp_b2[i][:, None, :],
                     n * TOP_K)
        x2 = _final(x1, ycomb, p[i].reshape(n, PLE_DIM), row(ple_norm_g[i]), ple_gate_w[i].astype(BF16),
                    ple_proj_w[i].astype(BF16), row(final_norm_g), i == depth - 1)
    return x2.reshape(b, s, d)
```

```python
import functools

import jax
import jax.numpy as jnp
from jax import lax
from jax.experimental import pallas as pl
from jax.experimental.pallas import tpu as pltpu

F32 = jnp.float32
BF16 = jnp.bfloat16

D_MODEL = 1024
HEAD_DIM = 64
SB_WIDTH = 512
RWKV_WIDTH = 512
DECAY_RANK = 64
ICLR_RANK = 64
GATE_RANK = 128
RWKV_COLS = 3 * RWKV_WIDTH + DECAY_RANK + ICLR_RANK + GATE_RANK
PLE_DIM = 256
N_EXPERTS = 32
TOP_K = 4
EXPERT_HIDDEN = D_MODEL
SWIGLU_LIMIT = 7.0
SWIGLU_ALPHA = 1.702
MOE_BLOCK = 256
MOE_HIDDEN_CHUNK = 256
RMS_EPS = 1e-5
GN_EPS = 64e-5

LANES = 128
TOKEN_TILE_ROWS = D_MODEL // LANES
ROW_TILE = 512
ATTN_BLOCK = 256
ATTN_HALVES = 2
CHUNK = 64
RWSCAN_BATCH = 4
RWSCAN_TILE = 256
VMEM_LIMIT = 56 * 1024 * 1024


def _params(sem, vmem=VMEM_LIMIT):
    return pltpu.CompilerParams(dimension_semantics=sem, vmem_limit_bytes=vmem)


def _dot(a, b):
    return jnp.dot(a.astype(BF16), b.astype(BF16), preferred_element_type=F32)


def _dot_nt(a, b):
    return lax.dot_general(a.astype(BF16), b.astype(BF16), (((1,), (1,)), ((), ())),
                           preferred_element_type=F32)


def _split(a):
    hi = a.astype(BF16)
    lo = (a - hi.astype(F32)).astype(BF16)
    return hi, lo


def _dot_exact_lhs(mask_bf16, a):
    hi, lo = _split(a)
    return (jnp.dot(mask_bf16, hi, preferred_element_type=F32)
            + jnp.dot(mask_bf16, lo, preferred_element_type=F32))


def _softplus(z):
    return jnp.maximum(z, 0.0) + jnp.log(1.0 + jnp.exp(-jnp.abs(z)))


def _sigmoid(z):
    return 1.0 / (1.0 + jnp.exp(-z))


def _rms(x, g):
    ms = jnp.mean(x * x, axis=-1, keepdims=True)
    return x * lax.rsqrt(ms + RMS_EPS) * g


def _inproj_kernel(x_ref, g_ref, wqkv_ref, wr_ref, wg_ref, qkv_ref, zr_ref, zg_ref):
    h = _rms(x_ref[...], g_ref[...]).astype(BF16)
    qkv_ref[...] = jnp.dot(h, wqkv_ref[...], preferred_element_type=F32).astype(BF16)
    zr_ref[...] = jnp.dot(h, wr_ref[...], preferred_element_type=F32)
    zg_ref[...] = jnp.dot(h, wg_ref[...], preferred_element_type=F32)


def _inproj(x2, g, wqkv, wr, wg):
    n = x2.shape[0]
    tm = ROW_TILE
    full = lambda a: pl.BlockSpec(a.shape, lambda i: (0,) * a.ndim)
    rows = lambda c: pl.BlockSpec((tm, c), lambda i: (i, 0))
    return pl.pallas_call(
        _inproj_kernel,
        grid=(n // tm,),
        in_specs=[rows(D_MODEL), full(g), full(wqkv), full(wr), full(wg)],
        out_specs=[rows(3 * SB_WIDTH), rows(RWKV_COLS), rows(2 * D_MODEL)],
        out_shape=[jax.ShapeDtypeStruct((n, 3 * SB_WIDTH), BF16),
                   jax.ShapeDtypeStruct((n, RWKV_COLS), F32),
                   jax.ShapeDtypeStruct((n, 2 * D_MODEL), F32)],
        compiler_params=_params(("parallel",)),
        name="inproj",
    )(x2, g, wqkv, wr, wg)


def _sbattn_kernel(q_ref, k_ref, v_ref, o_ref):
    tb = ATTN_BLOCK
    i = pl.program_id(2)
    lane = lax.broadcasted_iota(jnp.int32, (tb, LANES), 1)
    row = lax.broadcasted_iota(jnp.int32, (tb, tb), 0)
    col = lax.broadcasted_iota(jnp.int32, (tb, tb), 1)
    causal = col < row
    later = jnp.where(row > col, -1.0, 0.0).astype(BF16)

    def kv(j):
        at = pl.ds(pl.multiple_of(j * tb, tb), tb)
        return k_ref[0, at, :], v_ref[0, at, :]

    def blocks(qhs, kvs, states, diag):
        n = range(len(qhs))
        z = [lax.dot_general(qhs[c], kvs[c][0], (((1,), (1,)), ((), ())), preferred_element_type=F32) for c in n]
        sp = [_softplus(z[c]) for c in n]
        spm = [jnp.where(causal, sp[c], 0.0) if diag else sp[c] for c in n]
        after = [jnp.dot(spm[c].astype(BF16), later, preferred_element_type=F32) for c in n]
        att = [jnp.exp((z[c] - sp[c]) + after[c] + states[c][0]) for c in n]
        if diag:
            att = [jnp.where(causal, att[c], 0.0) for c in n]
        acc = [states[c][1] + jnp.dot(att[c].astype(BF16), kvs[c][1], preferred_element_type=F32) for c in n]
        carry = [states[c][0] - jnp.sum(spm[c], axis=1, keepdims=True) for c in n]
        return [(carry[c], acc[c]) for c in n]

    zero = (jnp.zeros((tb, 1), F32), jnp.zeros((tb, LANES), F32))
    qs, kv_diag = [], []
    for half in range(ATTN_HALVES):
        q = q_ref[0, half * tb:(half + 1) * tb, :]
        for head in range(2):
            in_head = (lane < HEAD_DIM) if head == 0 else (lane >= HEAD_DIM)
            qs.append(jnp.where(in_head, q, jnp.zeros_like(q)))
            kv_diag.append(kv(ATTN_HALVES * i + half))
    states = blocks(qs, kv_diag, [zero] * 4, True)
    states[2:] = blocks(qs[2:], [kv(ATTN_HALVES * i)] * 2, states[2:], False)

    def body(t, st):
        return tuple(blocks(qs, [kv(ATTN_HALVES * i - 1 - t)] * 4, list(st), False))

    states = lax.fori_loop(0, ATTN_HALVES * i, body, tuple(states))
    for half in range(ATTN_HALVES):
        o_ref[0, half * tb:(half + 1) * tb, :] = jnp.where(
            lane < HEAD_DIM, states[2 * half][1], states[2 * half + 1][1]).astype(BF16)


def _sbattn(qkv3):
    b, s, _ = qkv3.shape
    tb = ATTN_BLOCK * ATTN_HALVES
    pairs = SB_WIDTH // LANES
    return pl.pallas_call(
        _sbattn_kernel,
        grid=(b, pairs, s // tb),
        in_specs=[pl.BlockSpec((1, tb, LANES), lambda bb, p, i: (bb, i, p)),
                  pl.BlockSpec((1, s, LANES), lambda bb, p, i: (bb, 0, pairs + p)),
                  pl.BlockSpec((1, s, LANES), lambda bb, p, i: (bb, 0, 2 * pairs + p))],
        out_specs=pl.BlockSpec((1, tb, LANES), lambda bb, p, i: (bb, i, p)),
        out_shape=jax.ShapeDtypeStruct((b, s, SB_WIDTH), BF16),
        compiler_params=_params(("parallel", "parallel", "arbitrary")),
        name="sbattn",
    )(qkv3, qkv3, qkv3)


def _head_sums(t):
    lane = lax.broadcasted_iota(jnp.int32, t.shape, 1)
    low = lane < HEAD_DIM
    s_low = jnp.sum(jnp.where(low, t, 0.0), axis=1, keepdims=True)
    s_high = jnp.sum(jnp.where(low, 0.0, t), axis=1, keepdims=True)
    return jnp.where(low, s_low, s_high)


def _rwprep_kernel(z_ref, zp_ref, mu_ref, w0_ref, wup_ref, a0_ref, aup_ref, gup_ref, kk_ref, ka_ref,
                   r_ref, k_ref, v_ref, kkn_ref, a_ref, lw_ref, g_ref):
    ts = z_ref.shape[1]
    i = pl.program_id(1)
    z = z_ref[0]
    rowid = lax.broadcasted_iota(jnp.int32, z.shape, 0)
    last = jnp.where(i == 0, 0.0, zp_ref[0, 7:8, :])
    prev = jnp.where(rowid == 0, last, pltpu.roll(z, 1, 0))
    zs = z + mu_ref[...] * (prev - z)
    c1, c2, c3 = RWKV_WIDTH, 2 * RWKV_WIDTH, 3 * RWKV_WIDTH
    r, k, v = zs[:, :c1], zs[:, c1:c2], zs[:, c2:c3]
    xwa = zs[:, c3:c3 + LANES]
    xg = zs[:, c3 + LANES:]
    hp = lax.Precision.HIGHEST
    w_pre = w0_ref[...] + jnp.dot(jnp.tanh(xwa), wup_ref[...], precision=hp, preferred_element_type=F32)
    w_log = -_softplus(-w_pre) - 0.5
    lw = -jnp.exp(w_log)
    a = _sigmoid(a0_ref[...] + jnp.dot(xwa, aup_ref[...], precision=hp, preferred_element_type=F32))
    g = jnp.dot(_sigmoid(xg), gup_ref[...], precision=hp, preferred_element_type=F32)
    kk = k * kk_ref[...]
    for t in range(RWKV_WIDTH // LANES):
        sl = slice(t * LANES, (t + 1) * LANES)
        kt = kk[:, sl]
        nrm = jnp.sqrt(_head_sums(kt * kt))
        kkn_ref[0, :, sl] = kt / jnp.maximum(nrm, 1e-12)
    r_ref[0] = r
    k_ref[0] = k * (1.0 + (a - 1.0) * ka_ref[...])
    v_ref[0] = v
    a_ref[0] = a
    lw_ref[0] = lw
    g_ref[0] = g


def _rwprep(zr3, mu, w0, wup, a0, aup, gup, k_k, k_a):
    b, s, _ = zr3.shape
    ts = ROW_TILE
    full = lambda a: pl.BlockSpec(a.shape, lambda bb, i: (0,) * a.ndim)
    out = pl.BlockSpec((1, ts, RWKV_WIDTH), lambda bb, i: (bb, i, 0))
    return pl.pallas_call(
        _rwprep_kernel,
        grid=(b, s // ts),
        in_specs=[pl.BlockSpec((1, ts, RWKV_COLS), lambda bb, i: (bb, i, 0)),
                  pl.BlockSpec((1, 8, RWKV_COLS), lambda bb, i: (bb, jnp.maximum(i * (ts // 8) - 1, 0), 0)),
                  full(mu), full(w0), full(wup), full(a0), full(aup), full(gup), full(k_k), full(k_a)],
        out_specs=[out] * 7,
        out_shape=[jax.ShapeDtypeStruct((b, s, RWKV_WIDTH), F32)] * 7,
        compiler_params=_params(("parallel", "parallel")),
        name="rwprep",
    )(zr3, zr3, mu, w0, wup, a0, aup, gup, k_k, k_a)


def _rwscan_kernel(r_ref, k_ref, v_ref, kk_ref, a_ref, lw_ref, g_ref, lnw_ref, lnb_ref, rk_ref, o_ref, state_ref):
    c = CHUNK
    tile = r_ref.shape[1]
    pairs = RWKV_WIDTH // LANES

    @pl.when(pl.program_id(1) == 0)
    def _():
        state_ref[...] = jnp.zeros_like(state_ref)

    lane = lax.broadcasted_iota(jnp.int32, (c, LANES), 1)
    heads = (lane < HEAD_DIM, lane >= HEAD_DIM)
    trow = lax.broadcasted_iota(jnp.int32, (c, c), 0)
    tcol = lax.broadcasted_iota(jnp.int32, (c, c), 1)
    strict = tcol < trow
    incl = tcol <= trow
    cum = jnp.where(incl, 1.0, 0.0).astype(BF16)
    srow = lax.broadcasted_iota(jnp.int32, (LANES, LANES), 0) < HEAD_DIM
    scol = lax.broadcasted_iota(jnp.int32, (LANES, LANES), 1) < HEAD_DIM
    same_head = srow == scol
    eye = jnp.where(trow == tcol, 1.0, 0.0)
    prange = range(r_ref.shape[0] * pairs)
    chains = [(p, hm) for p in prange for hm in heads]

    def chunk(ci, _):
        sl = pl.ds(pl.multiple_of(ci * c, c), c)
        col = lambda p: slice((p % pairs) * LANES, (p % pairs + 1) * LANES)
        seq = lambda ref, p: ref[p // pairs, sl, col(p)]
        r = [seq(r_ref, p) for p in prange]
        k = [seq(k_ref, p) for p in prange]
        v = [seq(v_ref, p) for p in prange]
        kk = [seq(kk_ref, p) for p in prange]
        a = [seq(a_ref, p) for p in prange]
        lw = [seq(lw_ref, p) for p in prange]
        lg = [_dot_exact_lhs(cum, lw[p]) for p in prange]
        lg_end = [lg[p][c - 1:c, :] for p in prange]
        e_inv = [jnp.exp(-lg[p]) for p in prange]
        e_end = [jnp.exp(lg_end[p] - lg[p]) for p in prange]
        ab = [kk[p] * a[p] for p in prange]
        kk_t = [kk[p] * jnp.exp(lg[p] - lw[p]) for p in prange]
        r_t = [r[p] * jnp.exp(lg[p]) for p in prange]
        b_t = [ab[p] * e_inv[p] for p in prange]
        k_t = [k[p] * e_inv[p] for p in prange]
        b_end = [ab[p] * e_end[p] for p in prange]
        k_end = [k[p] * e_end[p] for p in prange]
        kk_h = [jnp.where(hm, kk_t[p], 0.0) for p, hm in chains]
        r_h = [jnp.where(hm, r_t[p], 0.0) for p, hm in chains]
        v_h = [jnp.where(hm, v[p], 0.0) for p, hm in chains]
        nc = range(len(chains))
        pair_of = [p for p, _ in chains]
        l_b = [jnp.where(strict, _dot_nt(kk_h[j], b_t[pair_of[j]]), 0.0) for j in nc]
        l_k = [jnp.where(strict, _dot_nt(kk_h[j], k_t[pair_of[j]]), 0.0) for j in nc]
        a_rb = [jnp.where(incl, _dot_nt(r_h[j], b_t[pair_of[j]]), 0.0) for j in nc]
        a_rk = [jnp.where(incl, _dot_nt(r_h[j], k_t[pair_of[j]]), 0.0) for j in nc]
        pw = [-l_b[j] for j in nc]
        t_inv = [eye + pw[j] for j in nc]
        for _ in range(c.bit_length() - 2):
            pw = [_dot(pw[j], pw[j]) for j in nc]
            t_inv = [t_inv[j] + _dot(t_inv[j], pw[j]) for j in nc]
        lkv = [_dot(l_k[j], v_h[j]) for j in nc]
        w_h = [_dot(t_inv[j], kk_h[j]) for j in nc]
        yv_h = [_dot(a_rk[j], v_h[j]) for j in nc]
        uv_h = [_dot(t_inv[j], lkv[j]) for j in nc]
        both = lambda xs, p: xs[2 * p] + xs[2 * p + 1]
        zs = [_dot_nt(jnp.concatenate([both(w_h, p), r_t[p]], axis=0), state_ref[p]) for p in prange]
        u = [-(zs[p][:c] + both(uv_h, p)) for p in prange]
        yu = [_dot(a_rb[j], jnp.where(chains[j][1], u[pair_of[j]], 0.0)) for j in nc]
        upd = [_dot(jnp.concatenate([u[p], v[p]], axis=0).T, jnp.concatenate([b_end[p], k_end[p]], axis=0))
               for p in prange]
        for p in prange:
            state_ref[p] = state_ref[p] * jnp.exp(lg_end[p]) + jnp.where(same_head, upd[p], 0.0)
        for p in prange:
            y = zs[p][c:] + both(yv_h, p) + both(yu, p)
            mean = _head_sums(y) * (1.0 / HEAD_DIM)
            d = y - mean
            var = _head_sums(d * d) * (1.0 / HEAD_DIM)
            yn = d * lax.rsqrt(var + GN_EPS) * lnw_ref[:, col(p)] + lnb_ref[:, col(p)]
            bonus = _head_sums(r[p] * k[p] * rk_ref[:, col(p)]) * v[p]
            o_ref[p // pairs, sl, col(p)] = ((yn + bonus) * seq(g_ref, p)).astype(o_ref.dtype)
        return 0

    lax.fori_loop(0, tile // c, chunk, 0)


def _rwscan(r, k, v, kk, a, lw, g, lnw, lnb, rk):
    b, s, _ = r.shape
    ts = RWSCAN_TILE
    pairs = RWKV_WIDTH // LANES
    nb = RWSCAN_BATCH
    seq = pl.BlockSpec((nb, ts, RWKV_WIDTH), lambda bb, i: (bb, i, 0))
    vec = pl.BlockSpec((1, RWKV_WIDTH), lambda bb, i: (0, 0))
    return pl.pallas_call(
        _rwscan_kernel,
        grid=(b // nb, s // ts),
        in_specs=[seq] * 7 + [vec] * 3,
        out_specs=seq,
        out_shape=jax.ShapeDtypeStruct((b, s, RWKV_WIDTH), BF16),
        scratch_shapes=[pltpu.VMEM((nb * pairs, LANES, LANES), F32)],
        compiler_params=_params(("parallel", "arbitrary")),
        name="rwscan",
    )(r, k, v, kk, a, lw, g, lnw, lnb, rk)


def _mixout_kernel(x_ref, ya_ref, yb_ref, zg_ref, woa_ref, wob_ref, wo_ref, fg_ref, rwt_ref, rb_ref,
                   x1_ref, h2_ref, idx_ref, gate_ref):
    ga = _sigmoid(zg_ref[:, :D_MODEL])
    gb = _sigmoid(zg_ref[:, D_MODEL:])
    merged = (ga * jnp.dot(ya_ref[...], woa_ref[...], preferred_element_type=F32)
              + gb * jnp.dot(yb_ref[...], wob_ref[...], preferred_element_type=F32))
    x1 = x_ref[...] + jnp.dot(merged.astype(BF16), wo_ref[...], preferred_element_type=F32)
    x1_ref[...] = x1
    h2 = _rms(x1, fg_ref[...])
    h2_ref[...] = h2.reshape(h2_ref.shape)
    logits = lax.dot_general(rwt_ref[...], h2, (((1,), (1,)), ((), ())),
                             precision=lax.Precision.HIGHEST, preferred_element_type=F32) + rb_ref[...]
    eid = lax.broadcasted_iota(jnp.int32, logits.shape, 0).astype(F32)
    vals, ids = [], []
    for _ in range(TOP_K):
        m = jnp.max(logits, axis=0, keepdims=True)
        sel = jnp.min(jnp.where(logits == m, eid, float(N_EXPERTS)), axis=0, keepdims=True)
        vals.append(m)
        ids.append(sel.astype(jnp.int32))
        logits = jnp.where(eid == sel, -jnp.inf, logits)
    ex = [jnp.exp(vv - vals[0]) for vv in vals]
    den = ex[0] + ex[1] + ex[2] + ex[3]
    idx_ref[...] = jnp.concatenate(ids, axis=0)
    gate_ref[...] = jnp.concatenate([e / den for e in ex], axis=0)


def _mixout(x2, ya, yb, zg, woa, wob, wo, fg, rwt, rb):
    n = x2.shape[0]
    tm = ROW_TILE
    full = lambda a: pl.BlockSpec(a.shape, lambda i: (0,) * a.ndim)
    rows = lambda c: pl.BlockSpec((tm, c), lambda i: (i, 0))
    cols = pl.BlockSpec((TOP_K, tm), lambda i: (0, i))
    return pl.pallas_call(
        _mixout_kernel,
        grid=(n // tm,),
        in_specs=[rows(D_MODEL), rows(SB_WIDTH), rows(RWKV_WIDTH), rows(2 * D_MODEL),
                  full(woa), full(wob), full(wo), full(fg), full(rwt), full(rb)],
        out_specs=[rows(D_MODEL), pl.BlockSpec((tm * TOKEN_TILE_ROWS, LANES), lambda i: (i, 0)), cols, cols],
        out_shape=[jax.ShapeDtypeStruct((n, D_MODEL), F32),
                   jax.ShapeDtypeStruct((n * TOKEN_TILE_ROWS, LANES), F32),
                   jax.ShapeDtypeStruct((TOP_K, n), jnp.int32),
                   jax.ShapeDtypeStruct((TOP_K, n), F32)],
        compiler_params=_params(("parallel",)),
        name="mixout",
    )(x2, ya, yb, zg, woa, wob, wo, fg, rwt, rb)


def _split_gate_up_kernel(w_ref, g_ref, u_ref):
    width = 2 * LANES
    rows = lax.broadcasted_iota(jnp.int32, (width, width), 0)
    cols = lax.broadcasted_iota(jnp.int32, (width, width), 1)
    src = jnp.where(cols < LANES, 2 * cols, 2 * (cols - LANES) + 1)
    sel = jnp.where(rows == src, 1.0, 0.0).astype(BF16)
    for c in range(w_ref.shape[2] // width):
        blk = w_ref[0, :, c * width:(c + 1) * width].astype(BF16)
        out = jnp.dot(blk, sel, preferred_element_type=F32)
        g_ref[0, :, c * LANES:(c + 1) * LANES] = out[:, :LANES].astype(BF16)
        u_ref[0, :, c * LANES:(c + 1) * LANES] = out[:, LANES:].astype(BF16)


def _split_gate_up(w1):
    e, d, f2 = w1.shape
    tr = 512
    out = pl.BlockSpec((1, tr, f2 // 2), lambda ee, i: (ee, i, 0))
    return pl.pallas_call(
        _split_gate_up_kernel,
        grid=(e, d // tr),
        in_specs=[pl.BlockSpec((1, tr, f2), lambda ee, i: (ee, i, 0))],
        out_specs=[out, out],
        out_shape=[jax.ShapeDtypeStruct((e, d, f2 // 2), BF16)] * 2,
        compiler_params=_params(("parallel", "parallel")),
        name="split_gate_up",
    )(w1)


def _moe_kernel(be_ref, nused_ref, tok0_ref, tokn_ref, dstp_ref, dstc_ref, wrow_ref, h_hbm,
                w1g_ref, w1u_ref, b1g_ref, b1u_ref, w2_ref, b2_ref, y_hbm,
                xbuf, ybuf, gsem, ssem):
    g = MOE_BLOCK
    i = pl.program_id(0)
    n_used = nused_ref[0]
    t = TOKEN_TILE_ROWS

    def gather(tok_ref, s):
        for r in range(g):
            src = h_hbm.at[pl.ds(pl.multiple_of(tok_ref[0, 0, r], t), t)]
            pltpu.make_async_copy(src, xbuf.at[s, pl.ds(r * t, t)], gsem.at[s]).start(priority=0)

    def scatter(dst_ref, s):
        for r in range(g):
            dst = y_hbm.at[pl.ds(pl.multiple_of(dst_ref[0, 0, r], t), t)]
            pltpu.make_async_copy(ybuf.at[s, pl.ds(r * t, t)], dst, ssem).start(priority=1)

    def slot_wait(buf, sem):
        pltpu.make_async_copy(buf, buf, sem).wait()

    @pl.when(i == 0)
    def _():
        ybuf[1] = jnp.zeros(ybuf.shape[1:], F32)
        gather(tok0_ref, 0)

    slot = lax.rem(i, 2)

    @pl.when(jnp.logical_and(i >= 1, i < n_used))
    def _():
        slot_wait(ybuf.at[slot], ssem)

    @pl.when(i < n_used)
    def _():
        slot_wait(xbuf.at[slot], gsem.at[slot])
        gather(tokn_ref, 1 - slot)
        scatter(dstp_ref, 1 - slot)
        xb = xbuf[slot].reshape(g, D_MODEL).astype(BF16)
        fc = MOE_HIDDEN_CHUNK
        n_chunks = EXPERT_HIDDEN // fc

        def up(j):
            cs = slice(j * fc, (j + 1) * fc)
            return (jnp.dot(xb, w1g_ref[0, :, cs], preferred_element_type=F32) + b1g_ref[0, :, cs],
                    jnp.dot(xb, w1u_ref[0, :, cs], preferred_element_type=F32) + b1u_ref[0, :, cs])

        def act(g_lin, u_lin):
            g_lin = jnp.minimum(g_lin, SWIGLU_LIMIT)
            u_lin = jnp.clip(u_lin, -SWIGLU_LIMIT, SWIGLU_LIMIT)
            glu = g_lin * _sigmoid(SWIGLU_ALPHA * g_lin)
            return ((u_lin + 1.0) * glu).astype(BF16)

        y = jnp.broadcast_to(b2_ref[0], (g, D_MODEL))
        pending = up(0)
        for j in range(n_chunks):
            ahead = up(j + 1) if j + 1 < n_chunks else None
            hmid = act(*pending)
            y = y + jnp.dot(hmid, w2_ref[0, j * fc:(j + 1) * fc, :], preferred_element_type=F32)
            pending = ahead
        rr = lax.broadcasted_iota(jnp.int32, (g, g), 0)
        cc = lax.broadcasted_iota(jnp.int32, (g, g), 1)
        wdiag = jnp.where(rr == cc, jnp.broadcast_to(wrow_ref[0], (g, g)), 0.0)
        wcol = jnp.dot(wdiag, jnp.ones((g, LANES), F32), precision=lax.Precision.HIGHEST,
                       preferred_element_type=F32)
        ybuf[slot] = (y * jnp.tile(wcol, (1, D_MODEL // LANES))).reshape(ybuf.shape[1:])

    @pl.when(i == n_used - 1)
    def _():
        slot_wait(ybuf.at[1 - slot], ssem)
        scatter(dstc_ref, slot)
        slot_wait(ybuf.at[slot], ssem)
        slot_wait(xbuf.at[1 - slot], gsem.at[1 - slot])


def _moe(block_e, n_used, tok, dst, wrow, h2, w1g, w1u, b1g, b1u, w2, b2, n_assign):
    nb = tok.shape[0]
    g = MOE_BLOCK
    d = D_MODEL
    f = EXPERT_HIDDEN
    t = TOKEN_TILE_ROWS
    spare = n_assign + jnp.arange(g, dtype=jnp.int32).reshape(1, 1, g)
    dst = jnp.concatenate([spare, dst], axis=0) * t
    tok = tok * t
    smem = lambda im: pl.BlockSpec((1, 1, g), im, memory_space=pltpu.SMEM)
    ew = lambda shape: pl.BlockSpec((1,) + shape, lambda i, be, nu: (be[i], 0, 0))
    grid_spec = pltpu.PrefetchScalarGridSpec(
        num_scalar_prefetch=2,
        grid=(nb,),
        in_specs=[smem(lambda i, be, nu: (0, 0, 0)),
                  smem(lambda i, be, nu: (jnp.minimum(i + 1, nb - 1), 0, 0)),
                  smem(lambda i, be, nu: (i, 0, 0)),
                  smem(lambda i, be, nu: (i + 1, 0, 0)),
                  pl.BlockSpec((1, 1, g), lambda i, be, nu: (i, 0, 0)),
                  pl.BlockSpec(memory_space=pl.ANY),
                  ew((d, f)), ew((d, f)), ew((1, f)), ew((1, f)), ew((f, d)), ew((1, d))],
        out_specs=pl.BlockSpec(memory_space=pl.ANY),
        scratch_shapes=[pltpu.VMEM((2, g * t, LANES), F32), pltpu.VMEM((2, g * t, LANES), F32),
                        pltpu.SemaphoreType.DMA((2,)), pltpu.SemaphoreType.DMA(())],
    )
    return pl.pallas_call(
        _moe_kernel,
        grid_spec=grid_spec,
        out_shape=jax.ShapeDtypeStruct(((n_assign + g) * t, LANES), F32),
        compiler_params=_params(("arbitrary",)),
        name="moe",
    )(block_e, n_used, tok, tok, dst, dst, wrow, h2, w1g, w1u, b1g, b1u, w2, b2)


def _final_kernel(x1_ref, y0_ref, y1_ref, y2_ref, y3_ref, p_ref, pg_ref, wgate_ref, wproj_ref, fg_ref, o_ref,
                  *, last):
    y = y0_ref[...] + y1_ref[...] + y2_ref[...] + y3_ref[...]
    x2 = x1_ref[...] + y.reshape(x1_ref.shape)
    hp = _rms(x2, pg_ref[...]).astype(BF16)
    gate = _sigmoid(jnp.dot(hp, wgate_ref[...], preferred_element_type=F32))
    proj = jnp.dot(p_ref[...].astype(BF16), wproj_ref[...], preferred_element_type=F32)
    x3 = x2 + gate * proj
    o_ref[...] = _rms(x3, fg_ref[...]) if last else x3


def _final(x1, ycomb, p2, pg, wgate, wproj, fg, last):
    n = x1.shape[0]
    tm = ROW_TILE
    full = lambda a: pl.BlockSpec(a.shape, lambda i: (0,) * a.ndim)
    rows = lambda c: pl.BlockSpec((tm, c), lambda i: (i, 0))
    ysp = lambda k: pl.BlockSpec((tm * TOKEN_TILE_ROWS, LANES), lambda i: (k * (n // tm) + i, 0))
    return pl.pallas_call(
        functools.partial(_final_kernel, last=last),
        grid=(n // tm,),
        in_specs=[rows(D_MODEL), ysp(0), ysp(1), ysp(2), ysp(3), rows(PLE_DIM),
                  full(pg), full(wgate), full(wproj), full(fg)],
        out_specs=rows(D_MODEL),
        out_shape=jax.ShapeDtypeStruct((n, D_MODEL), F32),
        compiler_params=_params(("parallel",)),
        name="final",
    )(x1, ycomb, ycomb, ycomb, ycomb, p2, pg, wgate, wproj, fg)


def _dispatch(top_idx, gate):
    n = top_idx.shape[1]
    nk = n * TOP_K
    i32 = jnp.int32
    flat_e = top_idx.reshape(-1)
    order = jnp.argsort(flat_e).astype(i32)
    experts = jnp.arange(N_EXPERTS, dtype=i32)
    counts = jnp.sum((flat_e[:, None] == experts[None, :]).astype(i32), axis=0)
    offsets = jnp.cumsum(counts) - counts
    padded = (counts + MOE_BLOCK - 1) // MOE_BLOCK * MOE_BLOCK
    pad_end = jnp.cumsum(padded)
    pad_off = pad_end - padded
    n_blocks = -(-nk // MOE_BLOCK) + N_EXPERTS
    block_start = jnp.arange(n_blocks, dtype=i32) * MOE_BLOCK
    block_e = jnp.minimum(jnp.sum((pad_end[None, :] <= block_start[:, None]).astype(i32), axis=1), N_EXPERTS - 1)
    within = jnp.arange(MOE_BLOCK, dtype=i32)[None, :]
    rank = (block_start - pad_off[block_e])[:, None] + within
    valid = rank < counts[block_e][:, None]
    assign = order[jnp.clip(offsets[block_e][:, None] + rank, 0, nk - 1)]
    buf_tok = jnp.where(valid, assign % n, 0)
    buf_dst = jnp.where(valid, assign, nk + within)
    buf_w = jnp.where(valid, gate.reshape(-1)[assign], 0.0)
    n_used = (pad_end[-1] // MOE_BLOCK).astype(i32).reshape(1)
    shape = (n_blocks, 1, MOE_BLOCK)
    return block_e, n_used, buf_tok.reshape(shape), buf_dst.reshape(shape), buf_w.reshape(shape)


def _pad_rows(w, before, total):
    return jnp.zeros((total, w.shape[1]), w.dtype).at[before:before + w.shape[0]].set(w)


def kernel(x, p, mix_norm_g, w_in, rwkv_mu, rwkv_w0, rwkv_w_up, rwkv_a0, rwkv_a_up, rwkv_g_up, rwkv_k_k, rwkv_k_a, rwkv_r_k, rwkv_lnx_w, rwkv_lnx_b, w_out_a, w_out_b, w_out, ffn_norm_g, router_w, router_b, exp_w1, exp_b1, exp_w2, exp_b2, ple_norm_g, ple_gate_w, ple_proj_w, final_norm_g):
    b, s, d = x.shape
    n = b * s
    depth = w_in.shape[0]
    row = lambda t: t.reshape(1, -1)
    x2 = x.reshape(n, d)
    for i in range(depth):
        c_sb = 3 * SB_WIDTH
        c_rw = c_sb + RWKV_COLS
        w = w_in[i]
        wqkv = jnp.concatenate([w[:, :SB_WIDTH] * HEAD_DIM ** -0.5, w[:, SB_WIDTH:c_sb]], axis=1).astype(BF16)
        qkv, zr, zg = _inproj(x2, row(mix_norm_g[i]), wqkv, w[:, c_sb:c_rw].astype(BF16), w[:, c_rw:].astype(BF16))
        ya = _sbattn(qkv.reshape(b, s, c_sb))
        wup = _pad_rows(rwkv_w_up[i], 0, LANES)
        aup = _pad_rows(rwkv_a_up[i], DECAY_RANK, LANES)
        r, kh, v, kkn, a, lw, g = _rwprep(
            zr.reshape(b, s, RWKV_COLS), row(rwkv_mu[i]), row(rwkv_w0[i]), wup, row(rwkv_a0[i]), aup,
            rwkv_g_up[i], row(rwkv_k_k[i]), row(rwkv_k_a[i]))
        yb = _rwscan(r, kh, v, kkn, a, lw, g, row(rwkv_lnx_w[i]), row(rwkv_lnx_b[i]), row(rwkv_r_k[i]))
        x1, h2, top_idx, gate = _mixout(
            x2, ya.reshape(n, SB_WIDTH), yb.reshape(n, RWKV_WIDTH), zg,
            w_out_a[i].astype(BF16), w_out_b[i].astype(BF16), w_out[i].astype(BF16), row(ffn_norm_g[i]),
            router_w[i].T, router_b[i].reshape(N_EXPERTS, 1))
        block_e, n_used, buf_tok, buf_dst, buf_w = _dispatch(top_idx, gate)
        w1g, w1u = _split_gate_up(exp_w1[i])
        b1 = exp_b1[i]
        ycomb = _moe(block_e, n_used, buf_tok, buf_dst, buf_w, h2, w1g, w1u,
                     b1[:, None, 0::2], b1[:, None, 1::2], exp_w2[i].astype(BF16), exp_b2[i][:, None, :],
                     n * TOP_K)
        x2 = _final(x1, ycomb, p[i].reshape(n, PLE_DIM), row(ple_norm_g[i]), ple_gate_w[i].astype(BF16),
                    ple_proj_w[i].astype(BF16), row(final_norm_g), i == depth - 1)
    return x2.reshape(b, s, d)
```
